```python
import math
import jax, jax.numpy as jnp
from jax import lax
import numpy as np

D_MODEL = 1024
BATCH = 8
SEQ = 2048
DEPTH = 1

HEAD_DIM = 64
MIX_WIDTH = D_MODEL
SG_WIDTH = MIX_WIDTH // 2
FOX_WIDTH = MIX_WIDTH - SG_WIDTH
SG_HEADS = SG_WIDTH // HEAD_DIM
FOX_HEADS = FOX_WIDTH // HEAD_DIM
CHUNK = 128
Q_BLOCK = 128
IN_COLS = 2 * SG_WIDTH + 3 * FOX_WIDTH + FOX_HEADS
PEER_HEADS = 8
N_KEYS = 128
N_EXPERTS = N_KEYS * N_KEYS
PEER_TOPK = 16
D_KEY = 256
D_HALF = D_KEY // 2
TOKEN_BLOCK = 128
EPS = 1e-6

kernel_name = "hybrid_sgu_fox_peer_adaln"


def rms_norm(x, g):
    xf = x.astype(jnp.float32)
    y = xf * lax.rsqrt(jnp.mean(xf * xf, axis=-1, keepdims=True) + EPS)
    return (y * g.astype(jnp.float32)).astype(x.dtype)


def head_rms_norm(x, g, n_heads):
    b, s, w = x.shape
    xf = x.astype(jnp.float32).reshape(b, s, n_heads, w // n_heads)
    y = xf * lax.rsqrt(jnp.mean(xf * xf, axis=-1, keepdims=True) + EPS)
    return (y.reshape(b, s, w) * g.astype(jnp.float32)).astype(x.dtype)


def modulate(h, shift, scale):
    return h * (1.0 + scale[:, None, :]) + shift[:, None, :]


def spatial_gating(z_u, z_v, ln_g, ln_b, w_s, b_s):
    b, s, _ = z_v.shape
    vf = z_v.astype(jnp.float32).reshape(b, s, SG_HEADS, HEAD_DIM)
    mu = jnp.mean(vf, axis=-1, keepdims=True)
    var = jnp.mean(jnp.square(vf - mu), axis=-1, keepdims=True)
    vn = (vf - mu) * lax.rsqrt(var + EPS)
    vn = vn * ln_g.astype(jnp.float32).reshape(SG_HEADS, HEAD_DIM) + ln_b.astype(jnp.float32).reshape(SG_HEADS, HEAD_DIM)
    vn = vn.astype(z_v.dtype).reshape(b, s // CHUNK, CHUNK, SG_HEADS, HEAD_DIM)
    causal = jnp.tril(jnp.ones((CHUNK, CHUNK), dtype=w_s.dtype))
    w = w_s * causal[None]
    mixed = jnp.einsum('hts,bcshd->bcthd', w, vn)
    mixed = mixed + jnp.transpose(b_s)[None, None, :, :, None]
    return z_u * mixed.reshape(b, s, SG_WIDTH)


def forgetting_attention(q, k, v, f_logit):
    b, s, h, dh = q.shape
    nb = s // Q_BLOCK
    cum = jnp.cumsum(jax.nn.log_sigmoid(f_logit.astype(jnp.float32)), axis=1)
    cum_k = jnp.transpose(cum, (0, 2, 1))
    q_blocks = q.reshape(b, nb, Q_BLOCK, h, dh).transpose(1, 0, 2, 3, 4)
    cum_q = cum.reshape(b, nb, Q_BLOCK, h).transpose(1, 0, 3, 2)
    k_pos = jnp.arange(s)
    scale = HEAD_DIM ** -0.5

    def one_block(args):
        qi, cqi, i = args
        logits = jnp.einsum('bqhd,bkhd->bhqk', qi, k).astype(jnp.float32) * scale
        logits = logits + cqi[..., :, None] - cum_k[:, :, None, :]
        q_pos = i * Q_BLOCK + jnp.arange(Q_BLOCK)
        mask = k_pos[None, :] <= q_pos[:, None]
        logits = jnp.where(mask[None, None], logits, -jnp.inf)
        p = jax.nn.softmax(logits, axis=-1).astype(v.dtype)
        return jnp.einsum('bhqk,bkhd->bqhd', p, v)

    out = lax.map(one_block, (q_blocks, cum_q, jnp.arange(nb)))
    return out.transpose(1, 0, 2, 3, 4).reshape(b, s, h * dh)


def token_mixer(h, w_in, b_f, sg_ln_g, sg_ln_b, sg_w, sg_b, out_g_sg, out_g_fox, w_out):
    b, s, _ = h.shape
    proj = h @ w_in
    c0 = SG_WIDTH
    c1 = 2 * SG_WIDTH
    c2 = c1 + FOX_WIDTH
    c3 = c2 + FOX_WIDTH
    c4 = c3 + FOX_WIDTH
    z_u = jax.nn.gelu(proj[..., :c0], approximate=False)
    z_v = jax.nn.gelu(proj[..., c0:c1], approximate=False)
    q = proj[..., c1:c2].reshape(b, s, FOX_HEADS, HEAD_DIM)
    k = proj[..., c2:c3].reshape(b, s, FOX_HEADS, HEAD_DIM)
    v = proj[..., c3:c4].reshape(b, s, FOX_HEADS, HEAD_DIM)
    f_logit = proj[..., c4:] + b_f
    y_sg = spatial_gating(z_u, z_v, sg_ln_g, sg_ln_b, sg_w, sg_b)
    y_fox = forgetting_attention(q, k, v, f_logit)
    y = jnp.concatenate([head_rms_norm(y_sg, out_g_sg, SG_HEADS),
                         head_rms_norm(y_fox, out_g_fox, FOX_HEADS)], axis=-1)
    return y @ w_out


def peer_ffn(h, w_query, sub_keys, expert_u, expert_v):
    b, s, d = h.shape
    t = b * s
    xt = h.reshape(t, d)
    q = (xt @ w_query).reshape(t, PEER_HEADS, 2, D_HALF)
    scores = jnp.einsum('thpd,hpnd->thpn', q, sub_keys).astype(jnp.float32)
    sv, si = lax.top_k(scores, PEER_TOPK)
    cand_s = (sv[:, :, 0, :, None] + sv[:, :, 1, None, :]).reshape(t, PEER_HEADS, PEER_TOPK * PEER_TOPK)
    cand_i = (si[:, :, 0, :, None] * N_KEYS + si[:, :, 1, None, :]).reshape(t, PEER_HEADS, PEER_TOPK * PEER_TOPK)
    top_s, pos = lax.top_k(cand_s, PEER_TOPK)
    idx = jnp.take_along_axis(cand_i, pos, axis=-1)
    gate = jax.nn.softmax(top_s, axis=-1).astype(h.dtype)
    nblk = t // TOKEN_BLOCK
    n_sel = PEER_HEADS * PEER_TOPK
    x_blocks = xt.reshape(nblk, TOKEN_BLOCK, d)
    idx_blocks = idx.reshape(nblk, TOKEN_BLOCK, n_sel)
    gate_blocks = gate.reshape(nblk, TOKEN_BLOCK, n_sel)

    def one_block(args):
        xi, ii, gi = args
        u = jnp.take(expert_u, ii, axis=0)
        a = jnp.einsum('ted,td->te', u, xi)
        w = gi * jax.nn.gelu(a, approximate=False)
        vv = jnp.take(expert_v, ii, axis=0)
        return jnp.einsum('te,ted->td', w, vv)

    out = lax.map(one_block, (x_blocks, idx_blocks, gate_blocks))
    return out.reshape(b, s, d)


def setup_inputs(seed: int = 0) -> dict:
    key = jax.random.key(seed)
    ks = jax.random.split(key, 24)
    f32 = jnp.float32
    D = D_MODEL

    def nrm(k, shape, scale):
        return jax.random.normal(k, shape, f32) * scale

    x = jax.random.normal(ks[0], (BATCH, SEQ, D), f32)
    c = jax.random.normal(ks[1], (BATCH, D), f32)
    w_ada = nrm(ks[2], (DEPTH, D, 6 * D), 0.5 * D ** -0.5)
    b_ada = nrm(ks[3], (DEPTH, 6 * D), 0.02)
    norm1_g = 1.0 + nrm(ks[4], (DEPTH, D), 0.02)
    w_in = nrm(ks[5], (DEPTH, D, IN_COLS), D ** -0.5)
    b_f = jax.random.uniform(ks[6], (DEPTH, FOX_HEADS), f32, minval=1.0, maxval=5.0)
    sg_ln_g = 1.0 + nrm(ks[7], (DEPTH, SG_WIDTH), 0.02)
    sg_ln_b = nrm(ks[8], (DEPTH, SG_WIDTH), 0.02)
    sg_w = nrm(ks[9], (DEPTH, SG_HEADS, CHUNK, CHUNK), CHUNK ** -0.5)
    sg_b = 1.0 + nrm(ks[10], (DEPTH, SG_HEADS, CHUNK), 0.1)
    out_g_sg = 1.0 + nrm(ks[11], (DEPTH, SG_WIDTH), 0.02)
    out_g_fox = 1.0 + nrm(ks[12], (DEPTH, FOX_WIDTH), 0.02)
    w_out = nrm(ks[13], (DEPTH, MIX_WIDTH, D), MIX_WIDTH ** -0.5)
    norm2_g = 1.0 + nrm(ks[14], (DEPTH, D), 0.02)
    w_query = nrm(ks[15], (DEPTH, D, PEER_HEADS * D_KEY), D ** -0.5)
    sub_keys = nrm(ks[16], (DEPTH, PEER_HEADS, 2, N_KEYS, D_HALF), D_HALF ** -0.5)
    expert_u = nrm(ks[17], (DEPTH, N_EXPERTS, D), D ** -0.5)
    expert_v = nrm(ks[18], (DEPTH, N_EXPERTS, D), PEER_HEADS ** -0.5)
    final_g = 1.0 + nrm(ks[19], (D,), 0.02)
    return {"x": x, "c": c, "w_ada": w_ada, "b_ada": b_ada, "norm1_g": norm1_g,
            "w_in": w_in, "b_f": b_f, "sg_ln_g": sg_ln_g, "sg_ln_b": sg_ln_b,
            "sg_w": sg_w, "sg_b": sg_b, "out_g_sg": out_g_sg, "out_g_fox": out_g_fox,
            "w_out": w_out, "norm2_g": norm2_g, "w_query": w_query, "sub_keys": sub_keys,
            "expert_u": expert_u, "expert_v": expert_v, "final_g": final_g}


def reference(x, c, w_ada, b_ada, norm1_g, w_in, b_f, sg_ln_g, sg_ln_b, sg_w, sg_b,
              out_g_sg, out_g_fox, w_out, norm2_g, w_query, sub_keys, expert_u, expert_v,
              final_g):
    c_act = jax.nn.silu(c)
    for l in range(DEPTH):
        ada = c_act @ w_ada[l] + b_ada[l]
        shift1, scale1, gate1, shift2, scale2, gate2 = jnp.split(ada, 6, axis=-1)
        h = modulate(rms_norm(x, norm1_g[l]), shift1, scale1)
        mix = token_mixer(h, w_in[l], b_f[l], sg_ln_g[l], sg_ln_b[l], sg_w[l], sg_b[l],
                          out_g_sg[l], out_g_fox[l], w_out[l])
        x = x + gate1[:, None, :] * mix
        h = modulate(rms_norm(x, norm2_g[l]), shift2, scale2)
        ffn = peer_ffn(h, w_query[l], sub_keys[l], expert_u[l], expert_v[l])
        x = x + gate2[:, None, :] * ffn
    return rms_norm(x, final_g)
```

```python
import functools
import math

import jax
import jax.numpy as jnp
from jax import lax
from jax.experimental import pallas as pl
from jax.experimental.pallas import tpu as pltpu

F32 = jnp.float32
BF16 = jnp.bfloat16

HEAD_DIM = 64
N_HEADS = 8
GROUP_W = N_HEADS * HEAD_DIM
CHUNK = 128
PEER_HEADS = 8
N_KEYS = 128
TOPK = 16
EPS = 1e-6
NEG_BIG = -1e30
VMEM_LIMIT = 48 * 1024 * 1024

TM_PROJ = 512
TM_OUT = 256
TQ = 512
TL_SEL = 256
TM_FFN = 512
E_BLK = 1024
LANE = 128


def _gelu(x):
    return 0.5 * x * (1.0 + lax.erf(x * (1.0 / math.sqrt(2.0))))


def _seg_sum(x, seg):
    hi = x.astype(BF16)
    lo = (x - hi.astype(F32)).astype(BF16)
    return (jnp.dot(hi, seg, preferred_element_type=F32)
            + jnp.dot(lo, seg, preferred_element_type=F32))


def _dot_nt(a, b):
    return lax.dot_general(a, b, (((1,), (1,)), ((), ())), preferred_element_type=F32)


def _adaln_kernel(c_ref, w_ref, b_ref, o_ref):
    c = c_ref[...]
    c_act = c / (1.0 + jnp.exp(-c))
    o_ref[...] = jnp.dot(c_act, w_ref[...], preferred_element_type=F32,
                         precision=lax.Precision.HIGHEST) + b_ref[...]


def _adaln(c, w_ada, b_ada):
    bsz, d = c.shape
    n_out = w_ada.shape[1]
    return pl.pallas_call(
        _adaln_kernel,
        name="adaln",
        grid=(n_out // d,),
        in_specs=[pl.BlockSpec((bsz, d), lambda j: (0, 0)),
                  pl.BlockSpec((d, d), lambda j: (0, j)),
                  pl.BlockSpec((1, d), lambda j: (0, j))],
        out_specs=pl.BlockSpec((bsz, d), lambda j: (0, j)),
        out_shape=jax.ShapeDtypeStruct((bsz, n_out), F32),
        compiler_params=pltpu.CompilerParams(vmem_limit_bytes=VMEM_LIMIT),
    )(c, w_ada, b_ada.reshape(1, n_out))


def _inproj_kernel(x_ref, sc_ref, sh_ref, g1_ref, wuv_ref, wqkv_ref, wft_ref, bf_ref,
                   lng_ref, lnb_ref, wsg_ref, bsg_ref, gsg_ref, seg_ref,
                   ysg_ref, q_ref, k_ref, v_ref, lf_ref):
    x = x_ref[...]
    ms = jnp.mean(x * x, axis=-1, keepdims=True)
    h = x * lax.rsqrt(ms + EPS) * g1_ref[...]
    h = h * (1.0 + sc_ref[0]) + sh_ref[0]
    hb = h.astype(BF16)

    uv = jnp.dot(hb, wuv_ref[...], preferred_element_type=F32)
    zu = _gelu(uv[:, :GROUP_W])
    zv = _gelu(uv[:, GROUP_W:])
    seg = seg_ref[...]
    inv_hd = 1.0 / HEAD_DIM
    mu = _seg_sum(zv, seg) * inv_hd
    dv = zv - mu
    var = _seg_sum(dv * dv, seg) * inv_hd
    vn = (dv * lax.rsqrt(var + EPS) * lng_ref[...] + lnb_ref[...]).astype(BF16)

    lane_head = lax.broadcasted_iota(jnp.int32, (CHUNK, GROUP_W), 1) // HEAD_DIM
    wsg = wsg_ref[...]
    for c in range(x.shape[0] // CHUNK):
        rows = slice(c * CHUNK, (c + 1) * CHUNK)
        r = jnp.dot(wsg, vn[rows], preferred_element_type=F32)
        mixed = bsg_ref[...]
        for hh in range(N_HEADS):
            mixed = mixed + jnp.where(lane_head == hh, r[hh * CHUNK:(hh + 1) * CHUNK], 0.0)
        y = zu[rows] * mixed
        msy = _seg_sum(y * y, seg) * inv_hd
        ysg_ref[rows, :] = (y * lax.rsqrt(msy + EPS) * gsg_ref[...]).astype(BF16)

    qkv = jnp.dot(hb, wqkv_ref[...], preferred_element_type=F32)
    q_ref[...] = (qkv[:, :GROUP_W] * (HEAD_DIM ** -0.5)).astype(BF16)
    k_ref[...] = qkv[:, GROUP_W:2 * GROUP_W].astype(BF16)
    v_ref[...] = qkv[:, 2 * GROUP_W:].astype(BF16)

    f = _dot_nt(wft_ref[...], hb) + bf_ref[...]
    lf_ref[...] = jnp.minimum(f, 0.0) - jnp.log1p(jnp.exp(-jnp.abs(f)))


def _inproj(x2, scale1, shift1, g1, w_uv, w_qkv, w_ft, b_f16, ln_g, ln_b, w_sg, b_sg, g_sg,
            seg, seq):
    t, d = x2.shape
    tm = TM_PROJ
    per_b = seq // tm
    full = lambda a: pl.BlockSpec(a.shape, lambda i: (0,) * a.ndim)
    tok = lambda w: pl.BlockSpec((tm, w), lambda i: (i, 0))
    mod = pl.BlockSpec((1, 1, d), lambda i: (i // per_b, 0, 0))
    return pl.pallas_call(
        _inproj_kernel,
        name="inproj",
        grid=(t // tm,),
        in_specs=[tok(d), mod, mod, full(g1), full(w_uv), full(w_qkv), full(w_ft), full(b_f16),
                  full(ln_g), full(ln_b), full(w_sg), full(b_sg), full(g_sg), full(seg)],
        out_specs=[tok(GROUP_W), tok(GROUP_W), tok(GROUP_W), tok(GROUP_W),
                   pl.BlockSpec((16, tm), lambda i: (0, i))],
        out_shape=[jax.ShapeDtypeStruct((t, GROUP_W), BF16)] * 4
                  + [jax.ShapeDtypeStruct((16, t), F32)],
        compiler_params=pltpu.CompilerParams(vmem_limit_bytes=VMEM_LIMIT),
    )(x2, scale1, shift1, g1, w_uv, w_qkv, w_ft, b_f16, ln_g, ln_b, w_sg, b_sg, g_sg, seg)


def _cumsum_kernel(lf_ref, o_ref):
    x = lf_ref[...]
    n = x.shape[1]
    lane = lax.broadcasted_iota(jnp.int32, x.shape, 1)
    k = 1
    while k < n:
        x = x + jnp.where(lane >= k, pltpu.roll(x, k, 1), 0.0)
        k *= 2
    o_ref[...] = -x


def _neg_cumsum(lf, seq):
    r, t = lf.shape
    return pl.pallas_call(
        _cumsum_kernel,
        name="cumsum",
        grid=(t // seq,),
        in_specs=[pl.BlockSpec((r, seq), lambda b: (0, b))],
        out_specs=pl.BlockSpec((r, seq), lambda b: (0, b)),
        out_shape=jax.ShapeDtypeStruct((r, t), F32),
    )(lf)


def _fox_kernel(q_ref, k_ref, v_ref, nc_ref, g_ref, seg_ref, o_ref):
    p = pl.program_id(1)
    qi = pl.program_id(2)
    tq = q_ref.shape[0]
    q = q_ref[...]
    lane = lax.broadcasted_iota(jnp.int32, (1, LANE), 1)
    first = lane < HEAD_DIM
    zero = jnp.zeros_like(q)
    q_heads = (jnp.where(first, q, zero), jnp.where(first, zero, q))
    rows = qi * tq + lax.broadcasted_iota(jnp.int32, (tq, 1), 0)
    col0 = lax.broadcasted_iota(jnp.int32, (1, tq), 1)

    def body(kb, carry):
        start = pl.multiple_of(kb * tq, tq)
        kblk = k_ref[pl.ds(start, tq), :]
        vblk = v_ref[pl.ds(start, tq), :]
        causal = (col0 + start) <= rows
        new = []
        for j in range(2):
            m, l, acc = carry[j]
            bias = nc_ref[pl.ds(2 * p + j, 1), pl.ds(start, tq)]
            s = _dot_nt(q_heads[j], kblk) + bias
            s = jnp.where(causal, s, NEG_BIG)
            m_new = jnp.maximum(m, jnp.max(s, axis=1, keepdims=True))
            alpha = jnp.exp(m - m_new)
            pe = jnp.exp(s - m_new)
            l = alpha * l + jnp.sum(pe, axis=1, keepdims=True)
            acc = alpha * acc + jnp.dot(pe.astype(BF16), vblk, preferred_element_type=F32)
            new.append((m_new, l, acc))
        return tuple(new)

    init = tuple((jnp.full((tq, 1), NEG_BIG, F32), jnp.zeros((tq, 1), F32),
                  jnp.zeros((tq, LANE), F32)) for _ in range(2))
    (_, l0, a0), (_, l1, a1) = lax.fori_loop(0, qi + 1, body, init)
    o = jnp.where(first, a0 / l0, a1 / l1)
    ms = _seg_sum(o * o, seg_ref[...]) * (1.0 / HEAD_DIM)
    o_ref[...] = (o * lax.rsqrt(ms + EPS) * g_ref[...]).astype(BF16)


def _fox(q, k, v, negc, g_fox, seg, bsz, seq):
    t = q.shape[0]
    nq = seq // TQ
    n_pairs = GROUP_W // LANE
    return pl.pallas_call(
        _fox_kernel,
        name="fox",
        grid=(bsz, n_pairs, nq),
        in_specs=[pl.BlockSpec((TQ, LANE), lambda b, p, i: (b * nq + i, p)),
                  pl.BlockSpec((seq, LANE), lambda b, p, i: (b, p)),
                  pl.BlockSpec((seq, LANE), lambda b, p, i: (b, p)),
                  pl.BlockSpec((negc.shape[0], seq), lambda b, p, i: (0, b)),
                  pl.BlockSpec((1, LANE), lambda b, p, i: (0, p)),
                  pl.BlockSpec((LANE, LANE), lambda b, p, i: (0, 0))],
        out_specs=pl.BlockSpec((TQ, LANE), lambda b, p, i: (b * nq + i, p)),
        out_shape=jax.ShapeDtypeStruct((t, GROUP_W), BF16),
        compiler_params=pltpu.CompilerParams(vmem_limit_bytes=VMEM_LIMIT),
    )(q, k, v, negc, g_fox, seg)


def _outproj_kernel(x_ref, ysg_ref, yfox_ref, gate_ref, sc_ref, sh_ref, g2_ref,
                    wo1_ref, wo2_ref, wq_ref, sk_ref, x1_ref, h2_ref, st_ref):
    mix = (jnp.dot(ysg_ref[...], wo1_ref[...], preferred_element_type=F32)
           + jnp.dot(yfox_ref[...], wo2_ref[...], preferred_element_type=F32))
    x1 = x_ref[...] + gate_ref[0] * mix
    x1_ref[...] = x1
    ms = jnp.mean(x1 * x1, axis=-1, keepdims=True)
    h2 = x1 * lax.rsqrt(ms + EPS) * g2_ref[...]
    h2 = (h2 * (1.0 + sc_ref[0]) + sh_ref[0]).astype(BF16)
    h2_ref[...] = h2
    qp = jnp.dot(h2, wq_ref[...], preferred_element_type=F32).astype(BF16)
    for j in range(2 * PEER_HEADS):
        st_ref[j] = _dot_nt(sk_ref[j], qp[:, j * LANE:(j + 1) * LANE])


def _outproj(x2, ysg, yfox, gate1, scale2, shift2, g2, wo1, wo2, wq, sk, seq):
    t, d = x2.shape
    tm = TM_OUT
    per_b = seq // tm
    full = lambda a: pl.BlockSpec(a.shape, lambda i: (0,) * a.ndim)
    tok = lambda w: pl.BlockSpec((tm, w), lambda i: (i, 0))
    mod = pl.BlockSpec((1, 1, d), lambda i: (i // per_b, 0, 0))
    n_half = sk.shape[0]
    return pl.pallas_call(
        _outproj_kernel,
        name="outproj",
        grid=(t // tm,),
        in_specs=[tok(d), tok(GROUP_W), tok(GROUP_W), mod, mod, mod, full(g2),
                  full(wo1), full(wo2), full(wq), full(sk)],
        out_specs=[tok(d), tok(d), pl.BlockSpec((n_half, N_KEYS, tm), lambda i: (0, 0, i))],
        out_shape=[jax.ShapeDtypeStruct((t, d), F32), jax.ShapeDtypeStruct((t, d), BF16),
                   jax.ShapeDtypeStruct((n_half, N_KEYS, t), F32)],
        compiler_params=pltpu.CompilerParams(vmem_limit_bytes=VMEM_LIMIT),
    )(x2, ysg, yfox, gate1, scale2, shift2, g2, wo1, wo2, wq, sk)


def _top16(x):
    vals = []
    for _ in range(TOPK):
        mx = jnp.max(x, axis=0, keepdims=True)
        vals.append(mx)
        x = jnp.where(x == mx, -jnp.inf, x)
    return vals


def _peersel_kernel(st_ref, a_ref, s1_ref, b_ref):
    for h in range(PEER_HEADS):
        s1 = st_ref[2 * h]
        s2 = st_ref[2 * h + 1]
        sv1 = _top16(s1)
        sv2 = jnp.concatenate(_top16(s2), axis=0)
        cand = jnp.concatenate([v + sv2 for v in sv1], axis=0)
        tv = _top16(cand)
        tau = tv[TOPK - 1]
        z = jnp.zeros_like(tau)
        for v in tv:
            z = z + jnp.exp(v - tv[0])
        a_ref[h] = jnp.exp(s1 - sv1[0]) / z
        b_ref[h] = jnp.exp(s2 - sv2[0:1])
        s1_ref[h] = s1 - tau


def _peersel(st):
    n_half, n_keys, t = st.shape
    tl = TL_SEL
    out = jax.ShapeDtypeStruct((PEER_HEADS, n_keys, t), F32)
    ospec = pl.BlockSpec((PEER_HEADS, n_keys, tl), lambda i: (0, 0, i))
    return pl.pallas_call(
        _peersel_kernel,
        name="peersel",
        grid=(t // tl,),
        in_specs=[pl.BlockSpec((n_half, n_keys, tl), lambda i: (0, 0, i))],
        out_specs=[ospec, ospec, ospec],
        out_shape=[out, out, out],
        compiler_params=pltpu.CompilerParams(vmem_limit_bytes=VMEM_LIMIT),
    )(st)


def _peerffn_kernel(h2_ref, u_ref, vt_ref, a_ref, s1_ref, b_ref, st_ref, o_ref, wg_ref):
    e = pl.program_id(1)
    tm = h2_ref.shape[0]
    pre = _dot_nt(u_ref[...], h2_ref[...])
    for r in range(u_ref.shape[0] // N_KEYS):
        rows = slice(r * N_KEYS, (r + 1) * N_KEYS)
        for lc in range(tm // LANE):
            lanes = slice(lc * LANE, (lc + 1) * LANE)
            w = jnp.zeros((N_KEYS, LANE), F32)
            for h in range(PEER_HEADS):
                a_row = a_ref[h, r:r + 1, lanes]
                s_row = s1_ref[h, r:r + 1, lanes]
                prod = a_row * b_ref[h, :, lanes]
                keep = (s_row + st_ref[2 * h + 1, :, lanes]) >= 0.0
                w = w + jnp.where(keep, prod, 0.0)
            wg_ref[rows, lanes] = (w * _gelu(pre[rows, lanes])).astype(BF16)
    contrib = jnp.dot(vt_ref[...], wg_ref[...], preferred_element_type=F32)

    @pl.when(e == 0)
    def _():
        o_ref[...] = contrib

    @pl.when(e != 0)
    def _():
        o_ref[...] += contrib


def _peerffn(h2, u_b, vt_b, a, s1p, bm, st):
    t, d = h2.shape
    n_exp = u_b.shape[0]
    tm = TM_FFN
    sel = pl.BlockSpec((PEER_HEADS, N_KEYS, tm), lambda i, e: (0, 0, i))
    sel_row = pl.BlockSpec((PEER_HEADS, E_BLK // N_KEYS, tm), lambda i, e: (0, e, i))
    return pl.pallas_call(
        _peerffn_kernel,
        name="peerffn",
        grid=(t // tm, n_exp // E_BLK),
        in_specs=[pl.BlockSpec((tm, d), lambda i, e: (i, 0)),
                  pl.BlockSpec((E_BLK, d), lambda i, e: (e, 0)),
                  pl.BlockSpec((d, E_BLK), lambda i, e: (0, e)),
                  sel_row, sel_row, sel,
                  pl.BlockSpec((2 * PEER_HEADS, N_KEYS, tm), lambda i, e: (0, 0, i))],
        out_specs=pl.BlockSpec((d, tm), lambda i, e: (0, i)),
        out_shape=jax.ShapeDtypeStruct((d, t), F32),
        scratch_shapes=[pltpu.VMEM((E_BLK, tm), BF16)],
        compiler_params=pltpu.CompilerParams(
            dimension_semantics=("parallel", "arbitrary"), vmem_limit_bytes=VMEM_LIMIT),
    )(h2, u_b, vt_b, a, s1p, bm, st)


def _final_kernel(x1_ref, ft_ref, gate_ref, g_ref, o_ref, *, normalize):
    x = x1_ref[...] + gate_ref[0] * ft_ref[...].T
    if normalize:
        ms = jnp.mean(x * x, axis=-1, keepdims=True)
        x = x * lax.rsqrt(ms + EPS) * g_ref[...]
    o_ref[...] = x


def _final(x1, ffn_t, gate2, g, seq, normalize):
    t, d = x1.shape
    tm = TM_PROJ
    per_b = seq // tm
    return pl.pallas_call(
        functools.partial(_final_kernel, normalize=normalize),
        name="final",
        grid=(t // tm,),
        in_specs=[pl.BlockSpec((tm, d), lambda i: (i, 0)),
                  pl.BlockSpec((d, tm), lambda i: (0, i)),
                  pl.BlockSpec((1, 1, d), lambda i: (i // per_b, 0, 0)),
                  pl.BlockSpec((1, d), lambda i: (0, 0))],
        out_specs=pl.BlockSpec((tm, d), lambda i: (i, 0)),
        out_shape=jax.ShapeDtypeStruct((t, d), F32),
        compiler_params=pltpu.CompilerParams(vmem_limit_bytes=VMEM_LIMIT),
    )(x1, ffn_t, gate2, g)


def _block_diag_ones(width, group):
    idx = jnp.arange(width) // group
    return (idx[:, None] == idx[None, :]).astype(BF16)


def kernel(x, c, w_ada, b_ada, norm1_g, w_in, b_f, sg_ln_g, sg_ln_b, sg_w, sg_b, out_g_sg,
           out_g_fox, w_out, norm2_g, w_query, sub_keys, expert_u, expert_v, final_g):
    bsz, seq, d = x.shape
    t = bsz * seq
    depth = w_ada.shape[0]
    x2 = x.reshape(t, d)
    c_uv = 2 * GROUP_W
    c_qkv = c_uv + 3 * GROUP_W
    seg_w = _block_diag_ones(GROUP_W, HEAD_DIM)
    seg_pair = _block_diag_ones(LANE, HEAD_DIM)
    row = lambda a: a.reshape(1, -1)

    for l in range(depth):
        ada = _adaln(c, w_ada[l], b_ada[l])
        shift1, scale1, gate1, shift2, scale2, gate2 = [
            a.reshape(bsz, 1, d) for a in jnp.split(ada, 6, axis=-1)]

        w_in_b = w_in[l].astype(BF16)
        w_ft = jnp.zeros((16, d), BF16).at[:N_HEADS].set(w_in_b[:, c_qkv:].T)
        b_f16 = jnp.zeros((16, 1), F32).at[:N_HEADS, 0].set(b_f[l])
        causal = jnp.tril(jnp.ones((CHUNK, CHUNK), F32))
        w_sg = (sg_w[l] * causal[None]).reshape(N_HEADS * CHUNK, CHUNK).astype(BF16)
        b_sg = jnp.repeat(sg_b[l].T, HEAD_DIM, axis=1)

        ysg, q, k, v, lf = _inproj(
            x2, scale1, shift1, row(norm1_g[l]), w_in_b[:, :c_uv], w_in_b[:, c_uv:c_qkv], w_ft,
            b_f16, row(sg_ln_g[l]), row(sg_ln_b[l]), w_sg, b_sg, row(out_g_sg[l]), seg_w, seq)
        negc = _neg_cumsum(lf, seq)
        yfox = _fox(q, k, v, negc, row(out_g_fox[l]), seg_pair, bsz, seq)

        w_out_b = w_out[l].astype(BF16)
        sk = sub_keys[l].reshape(2 * PEER_HEADS, N_KEYS, -1).astype(BF16)
        x1, h2, st = _outproj(x2, ysg, yfox, gate1, scale2, shift2, row(norm2_g[l]),
                              w_out_b[:GROUP_W], w_out_b[GROUP_W:], w_query[l].astype(BF16),
                              sk, seq)
        a, s1p, bm = _peersel(st)
        ffn_t = _peerffn(h2, expert_u[l].astype(BF16), expert_v[l].T.astype(BF16),
                         a, s1p, bm, st)
        x2 = _final(x1, ffn_t, gate2, row(final_g), seq, normalize=(l == depth - 1))
    return x2.reshape(bsz, seq, d)
```

```python
import functools
import math

import jax
import jax.numpy as jnp
from jax import lax
from jax.experimental import pallas as pl
from jax.experimental.pallas import tpu as pltpu

F32 = jnp.float32
BF16 = jnp.bfloat16

HEAD_DIM = 64
N_HEADS = 8
GROUP_W = N_HEADS * HEAD_DIM
CHUNK = 128
PEER_HEADS = 8
N_KEYS = 128
TOPK = 16
EPS = 1e-6
NEG_BIG = -1e30
VMEM_LIMIT = 48 * 1024 * 1024

TM_PROJ = 512
TM_OUT = 256
TQ = 512
TL_SEL = 256
TM_FFN = 512
E_BLK = 1024
LANE = 128


def _gelu(x):
    return 0.5 * x * (1.0 + lax.erf(x * (1.0 / math.sqrt(2.0))))


def _seg_sum(x, seg):
    hi = x.astype(BF16)
    lo = (x - hi.astype(F32)).astype(BF16)
    return (jnp.dot(hi, seg, preferred_element_type=F32)
            + jnp.dot(lo, seg, preferred_element_type=F32))


def _dot_nt(a, b):
    return lax.dot_general(a, b, (((1,), (1,)), ((), ())), preferred_element_type=F32)


def _adaln_kernel(c_ref, w_ref, b_ref, o_ref):
    c = c_ref[...]
    c_act = c / (1.0 + jnp.exp(-c))
    o_ref[...] = jnp.dot(c_act, w_ref[...], preferred_element_type=F32,
                         precision=lax.Precision.HIGHEST) + b_ref[...]


def _adaln(c, w_ada, b_ada):
    bsz, d = c.shape
    n_out = w_ada.shape[1]
    return pl.pallas_call(
        _adaln_kernel,
        name="adaln",
        grid=(n_out // d,),
        in_specs=[pl.BlockSpec((bsz, d), lambda j: (0, 0)),
                  pl.BlockSpec((d, d), lambda j: (0, j)),
                  pl.BlockSpec((1, d), lambda j: (0, j))],
        out_specs=pl.BlockSpec((bsz, d), lambda j: (0, j)),
        out_shape=jax.ShapeDtypeStruct((bsz, n_out), F32),
        compiler_params=pltpu.CompilerParams(vmem_limit_bytes=VMEM_LIMIT),
    )(c, w_ada, b_ada.reshape(1, n_out))


def _inproj_kernel(x_ref, sc_ref, sh_ref, g1_ref, wuv_ref, wqkv_ref, wft_ref, bf_ref,
                   lng_ref, lnb_ref, wsg_ref, bsg_ref, gsg_ref, seg_ref,
                   ysg_ref, q_ref, k_ref, v_ref, lf_ref):
    x = x_ref[...]
    ms = jnp.mean(x * x, axis=-1, keepdims=True)
    h = x * lax.rsqrt(ms + EPS) * g1_ref[...]
    h = h * (1.0 + sc_ref[0]) + sh_ref[0]
    hb = h.astype(BF16)

    uv = jnp.dot(hb, wuv_ref[...], preferred_element_type=F32)
    zu = _gelu(uv[:, :GROUP_W])
    zv = _gelu(uv[:, GROUP_W:])
    seg = seg_ref[...]
    inv_hd = 1.0 / HEAD_DIM
    mu = _seg_sum(zv, seg) * inv_hd
    dv = zv - mu
    var = _seg_sum(dv * dv, seg) * inv_hd
    vn = (dv * lax.rsqrt(var + EPS) * lng_ref[...] + lnb_ref[...]).astype(BF16)

    lane_head = lax.broadcasted_iota(jnp.int32, (CHUNK, GROUP_W), 1) // HEAD_DIM
    wsg = wsg_ref[...]
    for c in range(x.shape[0] // CHUNK):
        rows = slice(c * CHUNK, (c + 1) * CHUNK)
        r = jnp.dot(wsg, vn[rows], preferred_element_type=F32)
        mixed = bsg_ref[...]
        for hh in range(N_HEADS):
            mixed = mixed + jnp.where(lane_head == hh, r[hh * CHUNK:(hh + 1) * CHUNK], 0.0)
        y = zu[rows] * mixed
        msy = _seg_sum(y * y, seg) * inv_hd
        ysg_ref[rows, :] = (y * lax.rsqrt(msy + EPS) * gsg_ref[...]).astype(BF16)

    qkv = jnp.dot(hb, wqkv_ref[...], preferred_element_type=F32)
    q_ref[...] = (qkv[:, :GROUP_W] * (HEAD_DIM ** -0.5)).astype(BF16)
    k_ref[...] = qkv[:, GROUP_W:2 * GROUP_W].astype(BF16)
    v_ref[...] = qkv[:, 2 * GROUP_W:].astype(BF16)

    f = _dot_nt(wft_ref[...], hb) + bf_ref[...]
    lf_ref[...] = jnp.minimum(f, 0.0) - jnp.log1p(jnp.exp(-jnp.abs(f)))


def _inproj(x2, scale1, shift1, g1, w_uv, w_qkv, w_ft, b_f16, ln_g, ln_b, w_sg, b_sg, g_sg,
            seg, seq):
    t, d = x2.shape
    tm = TM_PROJ
    per_b = seq // tm
    full = lambda a: pl.BlockSpec(a.shape, lambda i: (0,) * a.ndim)
    tok = lambda w: pl.BlockSpec((tm, w), lambda i: (i, 0))
    mod = pl.BlockSpec((1, 1, d), lambda i: (i // per_b, 0, 0))
    return pl.pallas_call(
        _inproj_kernel,
        name="inproj",
        grid=(t // tm,),
        in_specs=[tok(d), mod, mod, full(g1), full(w_uv), full(w_qkv), full(w_ft), full(b_f16),
                  full(ln_g), full(ln_b), full(w_sg), full(b_sg), full(g_sg), full(seg)],
        out_specs=[tok(GROUP_W), tok(GROUP_W), tok(GROUP_W), tok(GROUP_W),
                   pl.BlockSpec((16, tm), lambda i: (0, i))],
        out_shape=[jax.ShapeDtypeStruct((t, GROUP_W), BF16)] * 4
                  + [jax.ShapeDtypeStruct((16, t), F32)],
        compiler_params=pltpu.CompilerParams(vmem_limit_bytes=VMEM_LIMIT),
    )(x2, scale1, shift1, g1, w_uv, w_qkv, w_ft, b_f16, ln_g, ln_b, w_sg, b_sg, g_sg, seg)


def _cumsum_kernel(lf_ref, o_ref):
    x = lf_ref[...]
    n = x.shape[1]
    lane = lax.broadcasted_iota(jnp.int32, x.shape, 1)
    k = 1
    while k < n:
        x = x + jnp.where(lane >= k, pltpu.roll(x, k, 1), 0.0)
        k *= 2
    o_ref[...] = -x


def _neg_cumsum(lf, seq):
    r, t = lf.shape
    return pl.pallas_call(
        _cumsum_kernel,
        name="cumsum",
        grid=(t // seq,),
        in_specs=[pl.BlockSpec((r, seq), lambda b: (0, b))],
        out_specs=pl.BlockSpec((r, seq), lambda b: (0, b)),
        out_shape=jax.ShapeDtypeStruct((r, t), F32),
    )(lf)


def _fox_kernel(q_ref, k_ref, v_ref, nc_ref, g_ref, seg_ref, o_ref):
    p = pl.program_id(1)
    qi = pl.program_id(2)
    tq = q_ref.shape[0]
    q = q_ref[...]
    lane = lax.broadcasted_iota(jnp.int32, (1, LANE), 1)
    first = lane < HEAD_DIM
    zero = jnp.zeros_like(q)
    q_heads = (jnp.where(first, q, zero), jnp.where(first, zero, q))
    rows = qi * tq + lax.broadcasted_iota(jnp.int32, (tq, 1), 0)
    col0 = lax.broadcasted_iota(jnp.int32, (1, tq), 1)

    def body(kb, carry):
        start = pl.multiple_of(kb * tq, tq)
        kblk = k_ref[pl.ds(start, tq), :]
        vblk = v_ref[pl.ds(start, tq), :]
        causal = (col0 + start) <= rows
        new = []
        for j in range(2):
            m, l, acc = carry[j]
            bias = nc_ref[pl.ds(2 * p + j, 1), pl.ds(start, tq)]
            s = _dot_nt(q_heads[j], kblk) + bias
            s = jnp.where(causal, s, NEG_BIG)
            m_new = jnp.maximum(m, jnp.max(s, axis=1, keepdims=True))
            alpha = jnp.exp(m - m_new)
            pe = jnp.exp(s - m_new)
            l = alpha * l + jnp.sum(pe, axis=1, keepdims=True)
            acc = alpha * acc + jnp.dot(pe.astype(BF16), vblk, preferred_element_type=F32)
            new.append((m_new, l, acc))
        return tuple(new)

    init = tuple((jnp.full((tq, 1), NEG_BIG, F32), jnp.zeros((tq, 1), F32),
                  jnp.zeros((tq, LANE), F32)) for _ in range(2))
    (_, l0, a0), (_, l1, a1) = lax.fori_loop(0, qi + 1, body, init)
    o = jnp.where(first, a0 / l0, a1 / l1)
    ms = _seg_sum(o * o, seg_ref[...]) * (1.0 / HEAD_DIM)
    o_ref[...] = (o * lax.rsqrt(ms + EPS) * g_ref[...]).astype(BF16)


def _fox(q, k, v, negc, g_fox, seg, bsz, seq):
    t = q.shape[0]
    nq = seq // TQ
    n_pairs = GROUP_W // LANE
    return pl.pallas_call(
        _fox_kernel,
        name="fox",
        grid=(bsz, n_pairs, nq),
        in_specs=[pl.BlockSpec((TQ, LANE), lambda b, p, i: (b * nq + i, p)),
                  pl.BlockSpec((seq, LANE), lambda b, p, i: (b, p)),
                  pl.BlockSpec((seq, LANE), lambda b, p, i: (b, p)),
                  pl.BlockSpec((negc.shape[0], seq), lambda b, p, i: (0, b)),
                  pl.BlockSpec((1, LANE), lambda b, p, i: (0, p)),
                  pl.BlockSpec((LANE, LANE), lambda b, p, i: (0, 0))],
        out_specs=pl.BlockSpec((TQ, LANE), lambda b, p, i: (b * nq + i, p)),
        out_shape=jax.ShapeDtypeStruct((t, GROUP_W), BF16),
        compiler_params=pltpu.CompilerParams(vmem_limit_bytes=VMEM_LIMIT),
    )(q, k, v, negc, g_fox, seg)


def _outproj_kernel(x_ref, ysg_ref, yfox_ref, gate_ref, sc_ref, sh_ref, g2_ref,
                    wo1_ref, wo2_ref, wq_ref, sk_ref, x1_ref, h2_ref, st_ref):
    mix = (jnp.dot(ysg_ref[...], wo1_ref[...], preferred_element_type=F32)
           + jnp.dot(yfox_ref[...], wo2_ref[...], preferred_element_type=F32))
    x1 = x_ref[...] + gate_ref[0] * mix
    x1_ref[...] = x1
    ms = jnp.mean(x1 * x1, axis=-1, keepdims=True)
    h2 = x1 * lax.rsqrt(ms + EPS) * g2_ref[...]
    h2 = (h2 * (1.0 + sc_ref[0]) + sh_ref[0]).astype(BF16)
    h2_ref[...] = h2
    qp = jnp.dot(h2, wq_ref[...], preferred_element_type=F32).astype(BF16)
    for j in range(2 * PEER_HEADS):
        st_ref[j] = _dot_nt(sk_ref[j], qp[:, j * LANE:(j + 1) * LANE])


def _outproj(x2, ysg, yfox, gate1, scale2, shift2, g2, wo1, wo2, wq, sk, seq):
    t, d = x2.shape
    tm = TM_OUT
    per_b = seq // tm
    full = lambda a: pl.BlockSpec(a.shape, lambda i: (0,) * a.ndim)
    tok = lambda w: pl.BlockSpec((tm, w), lambda i: (i, 0))
    mod = pl.BlockSpec((1, 1, d), lambda i: (i // per_b, 0, 0))
    n_half = sk.shape[0]
    return pl.pallas_call(
        _outproj_kernel,
        name="outproj",
        grid=(t // tm,),
        in_specs=[tok(d), tok(GROUP_W), tok(GROUP_W), mod, mod, mod, full(g2),
                  full(wo1), full(wo2), full(wq), full(sk)],
        out_specs=[tok(d), tok(d), pl.BlockSpec((n_half, N_KEYS, tm), lambda i: (0, 0, i))],
        out_shape=[jax.ShapeDtypeStruct((t, d), F32), jax.ShapeDtypeStruct((t, d), BF16),
                   jax.ShapeDtypeStruct((n_half, N_KEYS, t), F32)],
        compiler_params=pltpu.CompilerParams(vmem_limit_bytes=VMEM_LIMIT),
    )(x2, ysg, yfox, gate1, scale2, shift2, g2, wo1, wo2, wq, sk)


def _ranked_top16(x, pos, tie_safe):
    n = x.shape[0]
    rank = jnp.full(x.shape, float(TOPK), F32)
    vals = []
    for r in range(TOPK):
        mx = jnp.max(x, axis=0, keepdims=True)
        hit = x == mx
        if tie_safe:
            first = jnp.min(jnp.where(hit, pos, float(n)), axis=0, keepdims=True)
            hit = pos == first
        vals.append(mx)
        x = jnp.where(hit, -jnp.inf, x)
        rank = jnp.where(hit, float(r), rank)
    taken = jnp.sum(jnp.where(rank < float(TOPK), 1.0, 0.0), axis=0, keepdims=True)
    return vals, rank, taken


_CAND_WIDTH = tuple(TOPK // (r + 1) for r in range(TOPK))
_CAND_ROWS = sum(_CAND_WIDTH)
_CAND_PAD = -(-_CAND_ROWS // 8) * 8


def _peersel_head(h, st_ref, a_ref, n_ref, b_ref, r2_ref, tie_safe):
    tl = st_ref.shape[2]
    pos_k = lax.broadcasted_iota(jnp.int32, (N_KEYS, tl), 0).astype(F32)
    pos_c = lax.broadcasted_iota(jnp.int32, (_CAND_PAD, tl), 0)
    row_c = jnp.zeros((_CAND_PAD, tl), jnp.int32)
    start = 0
    for wdt in _CAND_WIDTH[:-1]:
        start += wdt
        row_c = row_c + (pos_c >= start).astype(jnp.int32)
    pos_c = pos_c.astype(F32)
    pad = jnp.full((_CAND_PAD - _CAND_ROWS, tl), -jnp.inf, F32)
    s1 = st_ref[2 * h]
    s2 = st_ref[2 * h + 1]
    sv1, rank1, taken1 = _ranked_top16(s1, pos_k, tie_safe)
    sv2, rank2, taken2 = _ranked_top16(s2, pos_k, tie_safe)
    sv2 = jnp.concatenate(sv2, axis=0)
    cand = jnp.concatenate(
        [sv1[r] + sv2[:wdt] for r, wdt in enumerate(_CAND_WIDTH)] + [pad], axis=0)
    _, rank_c, taken_c = _ranked_top16(cand, pos_c, tie_safe)
    tied = jnp.zeros((1, tl), F32)
    for taken in (taken1, taken2, taken_c):
        tied = tied + jnp.where(taken != float(TOPK), 1.0, 0.0)
    chosen = rank_c < float(TOPK)
    z = jnp.sum(jnp.where(chosen, jnp.exp(cand - (sv1[0] + sv2[0:1])), 0.0),
                axis=0, keepdims=True)
    n_sel = jnp.zeros((N_KEYS, tl), F32)
    for r in range(TOPK):
        cnt = jnp.sum(jnp.where(chosen & (row_c == r), 1.0, 0.0), axis=0, keepdims=True)
        n_sel = jnp.where(rank1 == float(r), cnt, n_sel)
    a_ref[h] = jnp.exp(s1 - sv1[0]) / z
    n_ref[h] = n_sel
    b_ref[h] = jnp.exp(s2 - sv2[0:1]).astype(BF16)
    r2_ref[h] = rank2.astype(BF16)
    return tied


def _peersel_kernel(st_ref, a_ref, n_ref, b_ref, r2_ref):
    for h in range(PEER_HEADS):
        tied = _peersel_head(h, st_ref, a_ref, n_ref, b_ref, r2_ref, tie_safe=False)

        @pl.when(jnp.max(tied) > 0.0)
        def _():
            _peersel_head(h, st_ref, a_ref, n_ref, b_ref, r2_ref, tie_safe=True)


def _peersel(st):
    n_half, n_keys, t = st.shape
    tl = TL_SEL
    ospec = pl.BlockSpec((PEER_HEADS, n_keys, tl), lambda i: (0, 0, i))
    shape = (PEER_HEADS, n_keys, t)
    return pl.pallas_call(
        _peersel_kernel,
        name="peersel",
        grid=(t // tl,),
        in_specs=[pl.BlockSpec((n_half, n_keys, tl), lambda i: (0, 0, i))],
        out_specs=[ospec] * 4,
        out_shape=[jax.ShapeDtypeStruct(shape, F32), jax.ShapeDtypeStruct(shape, F32),
                   jax.ShapeDtypeStruct(shape, BF16), jax.ShapeDtypeStruct(shape, BF16)],
        compiler_params=pltpu.CompilerParams(vmem_limit_bytes=VMEM_LIMIT),
    )(st)


def _peerffn_kernel(h2_ref, u_ref, vt_ref, a_ref, n_ref, b_ref, r2_ref, o_ref,
                    wg_ref, bs_ref, rs_ref):
    e = pl.program_id(1)
    tm = h2_ref.shape[0]

    @pl.when(e == 0)
    def _():
        for h in range(PEER_HEADS):
            bs_ref[h] = b_ref[h]
            rs_ref[h] = r2_ref[h]

    pre = _dot_nt(u_ref[...], h2_ref[...])
    zero = jnp.zeros((N_KEYS, LANE), BF16)
    for r in range(u_ref.shape[0] // N_KEYS):
        rows = slice(r * N_KEYS, (r + 1) * N_KEYS)
        for lc in range(tm // LANE):
            lanes = slice(lc * LANE, (lc + 1) * LANE)
            w = zero
            for h in range(PEER_HEADS):
                a_row = jnp.broadcast_to(a_ref[h, r:r + 1, lanes], (N_KEYS, LANE)).astype(BF16)
                n_row = jnp.broadcast_to(n_ref[h, r:r + 1, lanes], (N_KEYS, LANE)).astype(BF16)
                keep = rs_ref[h, :, lanes] < n_row
                w = w + jnp.where(keep, a_row * bs_ref[h, :, lanes], zero)
            wg_ref[rows, lanes] = w * _gelu(pre[rows, lanes]).astype(BF16)
    contrib = jnp.dot(vt_ref[...], wg_ref[...], preferred_element_type=F32)

    @pl.when(e == 0)
    def _():
        o_ref[...] = contrib

    @pl.when(e != 0)
    def _():
        o_ref[...] += contrib


def _peerffn(h2, u_b, vt_b, a, n_sel, bm, r2):
    t, d = h2.shape
    n_exp = u_b.shape[0]
    tm = TM_FFN
    sel = pl.BlockSpec((PEER_HEADS, N_KEYS, tm), lambda i, e: (0, 0, i))
    sel_row = pl.BlockSpec((PEER_HEADS, E_BLK // N_KEYS, tm), lambda i, e: (0, e, i))
    return pl.pallas_call(
        _peerffn_kernel,
        name="peerffn",
        grid=(t // tm, n_exp // E_BLK),
        in_specs=[pl.BlockSpec((tm, d), lambda i, e: (i, 0)),
                  pl.BlockSpec((E_BLK, d), lambda i, e: (e, 0)),
                  pl.BlockSpec((d, E_BLK), lambda i, e: (0, e)),
                  sel_row, sel_row, sel, sel],
        out_specs=pl.BlockSpec((d, tm), lambda i, e: (0, i)),
        out_shape=jax.ShapeDtypeStruct((d, t), F32),
        scratch_shapes=[pltpu.VMEM((E_BLK, tm), BF16),
                        pltpu.VMEM((PEER_HEADS, N_KEYS, tm), BF16),
                        pltpu.VMEM((PEER_HEADS, N_KEYS, tm), BF16)],
        compiler_params=pltpu.CompilerParams(
            dimension_semantics=("arbitrary", "arbitrary"), vmem_limit_bytes=VMEM_LIMIT),
    )(h2, u_b, vt_b, a, n_sel, bm, r2)


def _final_kernel(x1_ref, ft_ref, gate_ref, g_ref, o_ref, *, normalize):
    x = x1_ref[...] + gate_ref[0] * ft_ref[...].T
    if normalize:
        ms = jnp.mean(x * x, axis=-1, keepdims=True)
        x = x * lax.rsqrt(ms + EPS) * g_ref[...]
    o_ref[...] = x


def _final(x1, ffn_t, gate2, g, seq, normalize):
    t, d = x1.shape
    tm = TM_PROJ
    per_b = seq // tm
    return pl.pallas_call(
        functools.partial(_final_kernel, normalize=normalize),
        name="final",
        grid=(t // tm,),
        in_specs=[pl.BlockSpec((tm, d), lambda i: (i, 0)),
                  pl.BlockSpec((d, tm), lambda i: (0, i)),
                  pl.BlockSpec((1, 1, d), lambda i: (i // per_b, 0, 0)),
                  pl.BlockSpec((1, d), lambda i: (0, 0))],
        out_specs=pl.BlockSpec((tm, d), lambda i: (i, 0)),
        out_shape=jax.ShapeDtypeStruct((t, d), F32),
        compiler_params=pltpu.CompilerParams(vmem_limit_bytes=VMEM_LIMIT),
    )(x1, ffn_t, gate2, g)


def _block_diag_ones(width, group):
    idx = jnp.arange(width) // group
    return (idx[:, None] == idx[None, :]).astype(BF16)


def kernel(x, c, w_ada, b_ada, norm1_g, w_in, b_f, sg_ln_g, sg_ln_b, sg_w, sg_b, out_g_sg,
           out_g_fox, w_out, norm2_g, w_query, sub_keys, expert_u, expert_v, final_g):
    bsz, seq, d = x.shape
    t = bsz * seq
    depth = w_ada.shape[0]
    x2 = x.reshape(t, d)
    c_uv = 2 * GROUP_W
    c_qkv = c_uv + 3 * GROUP_W
    seg_w = _block_diag_ones(GROUP_W, HEAD_DIM)
    seg_pair = _block_diag_ones(LANE, HEAD_DIM)
    row = lambda a: a.reshape(1, -1)

    for l in range(depth):
        ada = _adaln(c, w_ada[l], b_ada[l])
        shift1, scale1, gate1, shift2, scale2, gate2 = [
            a.reshape(bsz, 1, d) for a in jnp.split(ada, 6, axis=-1)]

        w_in_b = w_in[l].astype(BF16)
        w_ft = jnp.zeros((16, d), BF16).at[:N_HEADS].set(w_in_b[:, c_qkv:].T)
        b_f16 = jnp.zeros((16, 1), F32).at[:N_HEADS, 0].set(b_f[l])
        causal = jnp.tril(jnp.ones((CHUNK, CHUNK), F32))
        w_sg = (sg_w[l] * causal[None]).reshape(N_HEADS * CHUNK, CHUNK).astype(BF16)
        b_sg = jnp.repeat(sg_b[l].T, HEAD_DIM, axis=1)

        ysg, q, k, v, lf = _inproj(
            x2, scale1, shift1, row(norm1_g[l]), w_in_b[:, :c_uv], w_in_b[:, c_uv:c_qkv], w_ft,
            b_f16, row(sg_ln_g[l]), row(sg_ln_b[l]), w_sg, b_sg, row(out_g_sg[l]), seg_w, seq)
        negc = _neg_cumsum(lf, seq)
        yfox = _fox(q, k, v, negc, row(out_g_fox[l]), seg_pair, bsz, seq)

        w_out_b = w_out[l].astype(BF16)
        sk = sub_keys[l].reshape(2 * PEER_HEADS, N_KEYS, -1).astype(BF16)
        x1, h2, st = _outproj(x2, ysg, yfox, gate1, scale2, shift2, row(norm2_g[l]),
                              w_out_b[:GROUP_W], w_out_b[GROUP_W:], w_query[l].astype(BF16),
                              sk, seq)
        a, n_sel, bm, r2 = _peersel(st)
        ffn_t = _peerffn(h2, expert_u[l].astype(BF16), expert_v[l].T.astype(BF16),
                         a, n_sel, bm, r2)
        x2 = _final(x1, ffn_t, gate2, row(final_g), seq, normalize=(l == depth - 1))
    return x2.reshape(bsz, seq, d)
```

```python
import functools
import math

import jax
import jax.numpy as jnp
from jax import lax
from jax.experimental import pallas as pl
from jax.experimental.pallas import tpu as pltpu

F32 = jnp.float32
BF16 = jnp.bfloat16

HEAD_DIM = 64
N_HEADS = 8
GROUP_W = N_HEADS * HEAD_DIM
CHUNK = 128
PEER_HEADS = 8
N_KEYS = 128
TOPK = 16
EPS = 1e-6
NEG_BIG = -1e30
LOG2_E = math.log2(math.e)
VMEM_LIMIT = 48 * 1024 * 1024

TM_PROJ = 512
TM_OUT = 256
TQ = 512
TL_SEL = 256
TM_FFN = 512
E_BLK = 2048
LANE = 128


def _gelu(x):
    return 0.5 * x * (1.0 + lax.erf(x * (1.0 / math.sqrt(2.0))))


def _seg_sum(x, seg):
    hi = x.astype(BF16)
    lo = (x - hi.astype(F32)).astype(BF16)
    return (jnp.dot(hi, seg, preferred_element_type=F32)
            + jnp.dot(lo, seg, preferred_element_type=F32))


def _dot_nt(a, b):
    return lax.dot_general(a, b, (((1,), (1,)), ((), ())), preferred_element_type=F32)


def _adaln_kernel(c_ref, w_ref, b_ref, o_ref):
    c = c_ref[...]
    c_act = c / (1.0 + jnp.exp(-c))
    o_ref[...] = jnp.dot(c_act, w_ref[...], preferred_element_type=F32,
                         precision=lax.Precision.HIGHEST) + b_ref[...]


def _adaln(c, w_ada, b_ada):
    bsz, d = c.shape
    n_out = w_ada.shape[1]
    return pl.pallas_call(
        _adaln_kernel,
        name="adaln",
        grid=(n_out // d,),
        in_specs=[pl.BlockSpec((bsz, d), lambda j: (0, 0)),
                  pl.BlockSpec((d, d), lambda j: (0, j)),
                  pl.BlockSpec((1, d), lambda j: (0, j))],
        out_specs=pl.BlockSpec((bsz, d), lambda j: (0, j)),
        out_shape=jax.ShapeDtypeStruct((bsz, n_out), F32),
        compiler_params=pltpu.CompilerParams(vmem_limit_bytes=VMEM_LIMIT),
    )(c, w_ada, b_ada.reshape(1, n_out))


def _inproj_kernel(x_ref, sc_ref, sh_ref, g1_ref, wuv_ref, wqkv_ref, wft_ref, bf_ref,
                   lng_ref, lnb_ref, wsg_ref, bsg_ref, gsg_ref, seg_ref,
                   ysg_ref, q_ref, k_ref, v_ref, lf_ref):
    x = x_ref[...]
    ms = jnp.mean(x * x, axis=-1, keepdims=True)
    h = x * lax.rsqrt(ms + EPS) * g1_ref[...]
    h = h * (1.0 + sc_ref[0]) + sh_ref[0]
    hb = h.astype(BF16)

    uv = jnp.dot(hb, wuv_ref[...], preferred_element_type=F32)
    zu = _gelu(uv[:, :GROUP_W])
    zv = _gelu(uv[:, GROUP_W:])
    seg = seg_ref[...]
    inv_hd = 1.0 / HEAD_DIM
    mu = _seg_sum(zv, seg) * inv_hd
    dv = zv - mu
    var = _seg_sum(dv * dv, seg) * inv_hd
    vn = (dv * lax.rsqrt(var + EPS) * lng_ref[...] + lnb_ref[...]).astype(BF16)

    lane_head = lax.broadcasted_iota(jnp.int32, (CHUNK, GROUP_W), 1) // HEAD_DIM
    wsg = wsg_ref[...]
    for c in range(x.shape[0] // CHUNK):
        rows = slice(c * CHUNK, (c + 1) * CHUNK)
        r = jnp.dot(wsg, vn[rows], preferred_element_type=F32)
        mixed = bsg_ref[...]
        for hh in range(N_HEADS):
            mixed = mixed + jnp.where(lane_head == hh, r[hh * CHUNK:(hh + 1) * CHUNK], 0.0)
        y = zu[rows] * mixed
        msy = _seg_sum(y * y, seg) * inv_hd
        ysg_ref[rows, :] = (y * lax.rsqrt(msy + EPS) * gsg_ref[...]).astype(BF16)

    qkv = jnp.dot(hb, wqkv_ref[...], preferred_element_type=F32)
    q_ref[...] = (qkv[:, :GROUP_W] * (HEAD_DIM ** -0.5 * LOG2_E)).astype(BF16)
    k_ref[...] = qkv[:, GROUP_W:2 * GROUP_W].astype(BF16)
    v_ref[...] = qkv[:, 2 * GROUP_W:].astype(BF16)

    f = _dot_nt(wft_ref[...], hb) + bf_ref[...]
    lf_ref[...] = jnp.minimum(f, 0.0) - jnp.log1p(jnp.exp(-jnp.abs(f)))


def _inproj(x2, scale1, shift1, g1, w_uv, w_qkv, w_ft, b_f16, ln_g, ln_b, w_sg, b_sg, g_sg,
            seg, seq):
    t, d = x2.shape
    tm = TM_PROJ
    per_b = seq // tm
    full = lambda a: pl.BlockSpec(a.shape, lambda i: (0,) * a.ndim)
    tok = lambda w: pl.BlockSpec((tm, w), lambda i: (i, 0))
    mod = pl.BlockSpec((1, 1, d), lambda i: (i // per_b, 0, 0))
    return pl.pallas_call(
        _inproj_kernel,
        name="inproj",
        grid=(t // tm,),
        in_specs=[tok(d), mod, mod, full(g1), full(w_uv), full(w_qkv), full(w_ft), full(b_f16),
                  full(ln_g), full(ln_b), full(w_sg), full(b_sg), full(g_sg), full(seg)],
        out_specs=[tok(GROUP_W), tok(GROUP_W), tok(GROUP_W), tok(GROUP_W),
                   pl.BlockSpec((16, tm), lambda i: (0, i))],
        out_shape=[jax.ShapeDtypeStruct((t, GROUP_W), BF16)] * 4
                  + [jax.ShapeDtypeStruct((16, t), F32)],
        compiler_params=pltpu.CompilerParams(vmem_limit_bytes=VMEM_LIMIT),
    )(x2, scale1, shift1, g1, w_uv, w_qkv, w_ft, b_f16, ln_g, ln_b, w_sg, b_sg, g_sg, seg)


def _cumsum_kernel(lf_ref, o_ref):
    x = lf_ref[...]
    n = x.shape[1]
    lane = lax.broadcasted_iota(jnp.int32, x.shape, 1)
    k = 1
    while k < n:
        x = x + jnp.where(lane >= k, pltpu.roll(x, k, 1), 0.0)
        k *= 2
    o_ref[...] = x * (-LOG2_E)


def _neg_cumsum(lf, seq):
    r, t = lf.shape
    return pl.pallas_call(
        _cumsum_kernel,
        name="cumsum",
        grid=(t // seq,),
        in_specs=[pl.BlockSpec((r, seq), lambda b: (0, b))],
        out_specs=pl.BlockSpec((r, seq), lambda b: (0, b)),
        out_shape=jax.ShapeDtypeStruct((r, t), F32),
    )(lf)


def _fox_kernel(q_ref, k_ref, v_ref, nc_ref, g_ref, seg_ref, o_ref):
    p = pl.program_id(1)
    qi = pl.program_id(2)
    tq = q_ref.shape[0]
    q = q_ref[...]
    lane = lax.broadcasted_iota(jnp.int32, (1, LANE), 1)
    first = lane < HEAD_DIM
    zero = jnp.zeros_like(q)
    q_heads = (jnp.where(first, q, zero), jnp.where(first, zero, q))
    causal = (lax.broadcasted_iota(jnp.int32, (tq, tq), 1)
              <= lax.broadcasted_iota(jnp.int32, (tq, tq), 0))

    def block(kb, carry, masked):
        start = pl.multiple_of(kb * tq, tq)
        kblk = k_ref[pl.ds(start, tq), :]
        vblk = v_ref[pl.ds(start, tq), :]
        new = []
        for j in range(2):
            m, l, acc = carry[j]
            bias = nc_ref[pl.ds(2 * p + j, 1), pl.ds(start, tq)]
            s = _dot_nt(q_heads[j], kblk) + bias
            if masked:
                s = jnp.where(causal, s, NEG_BIG)
            m_new = jnp.maximum(m, jnp.max(s, axis=1, keepdims=True))
            alpha = jnp.exp2(m - m_new)
            pe = jnp.exp2(s - m_new)
            l = alpha * l + jnp.sum(pe, axis=1, keepdims=True)
            acc = alpha * acc + jnp.dot(pe.astype(BF16), vblk, preferred_element_type=F32)
            new.append((m_new, l, acc))
        return tuple(new)

    init = tuple((jnp.full((tq, 1), NEG_BIG, F32), jnp.zeros((tq, 1), F32),
                  jnp.zeros((tq, LANE), F32)) for _ in range(2))
    carry = lax.fori_loop(0, qi, lambda kb, c: block(kb, c, masked=False), init)
    (_, l0, a0), (_, l1, a1) = block(qi, carry, masked=True)
    o = jnp.where(first, a0 / l0, a1 / l1)
    ms = _seg_sum(o * o, seg_ref[...]) * (1.0 / HEAD_DIM)
    o_ref[...] = (o * lax.rsqrt(ms + EPS) * g_ref[...]).astype(BF16)


def _fox(q, k, v, negc, g_fox, seg, bsz, seq):
    t = q.shape[0]
    nq = seq // TQ
    n_pairs = GROUP_W // LANE
    return pl.pallas_call(
        _fox_kernel,
        name="fox",
        grid=(bsz, n_pairs, nq),
        in_specs=[pl.BlockSpec((TQ, LANE), lambda b, p, i: (b * nq + i, p)),
                  pl.BlockSpec((seq, LANE), lambda b, p, i: (b, p)),
                  pl.BlockSpec((seq, LANE), lambda b, p, i: (b, p)),
                  pl.BlockSpec((negc.shape[0], seq), lambda b, p, i: (0, b)),
                  pl.BlockSpec((1, LANE), lambda b, p, i: (0, p)),
                  pl.BlockSpec((LANE, LANE), lambda b, p, i: (0, 0))],
        out_specs=pl.BlockSpec((TQ, LANE), lambda b, p, i: (b * nq + i, p)),
        out_shape=jax.ShapeDtypeStruct((t, GROUP_W), BF16),
        compiler_params=pltpu.CompilerParams(vmem_limit_bytes=VMEM_LIMIT),
    )(q, k, v, negc, g_fox, seg)


def _outproj_kernel(x_ref, ysg_ref, yfox_ref, gate_ref, sc_ref, sh_ref, g2_ref,
                    wo1_ref, wo2_ref, wq_ref, sk_ref, x1_ref, h2_ref, st_ref):
    mix = (jnp.dot(ysg_ref[...], wo1_ref[...], preferred_element_type=F32)
           + jnp.dot(yfox_ref[...], wo2_ref[...], preferred_element_type=F32))
    x1 = x_ref[...] + gate_ref[0] * mix
    x1_ref[...] = x1
    ms = jnp.mean(x1 * x1, axis=-1, keepdims=True)
    h2 = x1 * lax.rsqrt(ms + EPS) * g2_ref[...]
    h2 = (h2 * (1.0 + sc_ref[0]) + sh_ref[0]).astype(BF16)
    h2_ref[...] = h2
    qp = jnp.dot(h2, wq_ref[...], preferred_element_type=F32).astype(BF16)
    for j in range(2 * PEER_HEADS):
        st_ref[j] = _dot_nt(sk_ref[j], qp[:, j * LANE:(j + 1) * LANE])


def _outproj(x2, ysg, yfox, gate1, scale2, shift2, g2, wo1, wo2, wq, sk, seq):
    t, d = x2.shape
    tm = TM_OUT
    per_b = seq // tm
    full = lambda a: pl.BlockSpec(a.shape, lambda i: (0,) * a.ndim)
    tok = lambda w: pl.BlockSpec((tm, w), lambda i: (i, 0))
    mod = pl.BlockSpec((1, 1, d), lambda i: (i // per_b, 0, 0))
    n_half = sk.shape[0]
    return pl.pallas_call(
        _outproj_kernel,
        name="outproj",
        grid=(t // tm,),
        in_specs=[tok(d), tok(GROUP_W), tok(GROUP_W), mod, mod, mod, full(g2),
                  full(wo1), full(wo2), full(wq), full(sk)],
        out_specs=[tok(d), tok(d), pl.BlockSpec((n_half, N_KEYS, tm), lambda i: (0, 0, i))],
        out_shape=[jax.ShapeDtypeStruct((t, d), F32), jax.ShapeDtypeStruct((t, d), BF16),
                   jax.ShapeDtypeStruct((n_half, N_KEYS, t), F32)],
        compiler_params=pltpu.CompilerParams(vmem_limit_bytes=VMEM_LIMIT),
    )(x2, ysg, yfox, gate1, scale2, shift2, g2, wo1, wo2, wq, sk)


def _ranked_top16(x, pos, tie_safe, want_rank=True):
    n = x.shape[0]
    rank = jnp.full(x.shape, float(TOPK), F32) if want_rank else None
    vals = []
    for r in range(TOPK):
        mx = jnp.max(x, axis=0, keepdims=True)
        hit = x == mx
        if tie_safe:
            first = jnp.min(jnp.where(hit, pos, float(n)), axis=0, keepdims=True)
            hit = pos == first
        vals.append(mx)
        x = jnp.where(hit, -jnp.inf, x)
        if want_rank:
            rank = jnp.where(hit, float(r), rank)
    taken = jnp.sum(jnp.where(x == -jnp.inf, 1.0, 0.0), axis=0, keepdims=True)
    return vals, rank, taken


_CAND_WIDTH = tuple(TOPK // (r + 1) for r in range(TOPK))
_CAND_ROWS = sum(_CAND_WIDTH)
_CAND_PAD = -(-_CAND_ROWS // 8) * 8


def _peersel_head(h, st_ref, a_ref, n_ref, b_ref, r2_ref, tie_safe):
    tl = st_ref.shape[2]
    pos_k = lax.broadcasted_iota(jnp.int32, (N_KEYS, tl), 0).astype(F32)
    pos_c = lax.broadcasted_iota(jnp.int32, (_CAND_PAD, tl), 0)
    row_c = jnp.zeros((_CAND_PAD, tl), jnp.int32)
    start = 0
    for wdt in _CAND_WIDTH[:-1]:
        start += wdt
        row_c = row_c + (pos_c >= start).astype(jnp.int32)
    pos_c = pos_c.astype(F32)
    pad = jnp.full((_CAND_PAD - _CAND_ROWS, tl), -jnp.inf, F32)
    s1 = st_ref[2 * h]
    s2 = st_ref[2 * h + 1]
    sv1, rank1, taken1 = _ranked_top16(s1, pos_k, tie_safe, want_rank=tie_safe)
    sv2, rank2, taken2 = _ranked_top16(s2, pos_k, tie_safe)
    sv2 = jnp.concatenate(sv2, axis=0)
    cand = jnp.concatenate(
        [sv1[r] + sv2[:wdt] for r, wdt in enumerate(_CAND_WIDTH)] + [pad], axis=0)
    _, rank_c, taken_c = _ranked_top16(cand, pos_c, tie_safe)
    tied = jnp.zeros((1, tl), F32)
    for taken, padding in ((taken1, 0), (taken2, 0), (taken_c, _CAND_PAD - _CAND_ROWS)):
        tied = tied + jnp.where(taken != float(TOPK + padding), 1.0, 0.0)
    chosen = rank_c < float(TOPK)
    z = jnp.sum(jnp.where(chosen, jnp.exp(cand - (sv1[0] + sv2[0:1])), 0.0),
                axis=0, keepdims=True)
    n_sel = jnp.zeros((N_KEYS, tl), F32)
    for r in range(TOPK):
        cnt = jnp.sum(jnp.where(chosen & (row_c == r), 1.0, 0.0), axis=0, keepdims=True)
        is_r = (rank1 == float(r)) if tie_safe else (s1 == sv1[r])
        n_sel = jnp.where(is_r, cnt, n_sel)
    a_ref[h] = jnp.exp(s1 - sv1[0]) / z
    n_ref[h] = n_sel
    b_ref[h] = jnp.exp(s2 - sv2[0:1]).astype(BF16)
    r2_ref[h] = rank2.astype(BF16)
    return tied


def _peersel_kernel(st_ref, a_ref, n_ref, b_ref, r2_ref):
    for h in range(PEER_HEADS):
        tied = _peersel_head(h, st_ref, a_ref, n_ref, b_ref, r2_ref, tie_safe=False)

        @pl.when(jnp.max(tied) > 0.0)
        def _():
            _peersel_head(h, st_ref, a_ref, n_ref, b_ref, r2_ref, tie_safe=True)


def _peersel(st):
    n_half, n_keys, t = st.shape
    tl = TL_SEL
    ospec = pl.BlockSpec((PEER_HEADS, n_keys, tl), lambda i: (0, 0, i))
    shape = (PEER_HEADS, n_keys, t)
    return pl.pallas_call(
        _peersel_kernel,
        name="peersel",
        grid=(t // tl,),
        in_specs=[pl.BlockSpec((n_half, n_keys, tl), lambda i: (0, 0, i))],
        out_specs=[ospec] * 4,
        out_shape=[jax.ShapeDtypeStruct(shape, F32), jax.ShapeDtypeStruct(shape, F32),
                   jax.ShapeDtypeStruct(shape, BF16), jax.ShapeDtypeStruct(shape, BF16)],
        compiler_params=pltpu.CompilerParams(vmem_limit_bytes=VMEM_LIMIT),
    )(st)


def _peerffn_kernel(h2_ref, u_ref, vt_ref, a_ref, n_ref, b_ref, r2_ref, x1_ref, gate_ref, g_ref,
                    o_ref, wg_ref, bs_ref, rs_ref, acc_ref, *, normalize):
    e = pl.program_id(1)
    tm = h2_ref.shape[0]

    @pl.when(e == 0)
    def _():
        for h in range(PEER_HEADS):
            bs_ref[h] = b_ref[h]
            rs_ref[h] = r2_ref[h]

    pre = _dot_nt(u_ref[...], h2_ref[...])
    zero = jnp.zeros((N_KEYS, LANE), BF16)
    for r in range(u_ref.shape[0] // N_KEYS):
        rows = slice(r * N_KEYS, (r + 1) * N_KEYS)
        for lc in range(tm // LANE):
            lanes = slice(lc * LANE, (lc + 1) * LANE)
            w = zero
            for h in range(PEER_HEADS):
                a_row = jnp.broadcast_to(a_ref[h, r:r + 1, lanes], (N_KEYS, LANE)).astype(BF16)
                n_row = jnp.broadcast_to(n_ref[h, r:r + 1, lanes], (N_KEYS, LANE)).astype(BF16)
                keep = rs_ref[h, :, lanes] < n_row
                w = w + jnp.where(keep, a_row * bs_ref[h, :, lanes], zero)
            wg_ref[rows, lanes] = w * _gelu(pre[rows, lanes]).astype(BF16)
    contrib = jnp.dot(vt_ref[...], wg_ref[...], preferred_element_type=F32)

    @pl.when(e == 0)
    def _():
        acc_ref[...] = contrib

    @pl.when(e != 0)
    def _():
        acc_ref[...] += contrib

    @pl.when(e == pl.num_programs(1) - 1)
    def _():
        x = x1_ref[...] + gate_ref[0] * acc_ref[...].T
        if normalize:
            ms = jnp.mean(x * x, axis=-1, keepdims=True)
            x = x * lax.rsqrt(ms + EPS) * g_ref[...]
        o_ref[...] = x


def _peerffn(h2, u_b, vt_b, a, n_sel, bm, r2, x1, gate2, g, seq, normalize):
    t, d = h2.shape
    n_exp = u_b.shape[0]
    tm = TM_FFN
    per_b = seq // tm
    sel = pl.BlockSpec((PEER_HEADS, N_KEYS, tm), lambda i, e: (0, 0, i))
    sel_row = pl.BlockSpec((PEER_HEADS, E_BLK // N_KEYS, tm), lambda i, e: (0, e, i))
    tok = pl.BlockSpec((tm, d), lambda i, e: (i, 0))
    return pl.pallas_call(
        functools.partial(_peerffn_kernel, normalize=normalize),
        name="peerffn",
        grid=(t // tm, n_exp // E_BLK),
        in_specs=[tok,
                  pl.BlockSpec((E_BLK, d), lambda i, e: (e, 0)),
                  pl.BlockSpec((d, E_BLK), lambda i, e: (0, e)),
                  sel_row, sel_row, sel, sel,
                  tok,
                  pl.BlockSpec((1, 1, d), lambda i, e: (i // per_b, 0, 0)),
                  pl.BlockSpec((1, d), lambda i, e: (0, 0))],
        out_specs=tok,
        out_shape=jax.ShapeDtypeStruct((t, d), F32),
        scratch_shapes=[pltpu.VMEM((E_BLK, tm), BF16),
                        pltpu.VMEM((PEER_HEADS, N_KEYS, tm), BF16),
                        pltpu.VMEM((PEER_HEADS, N_KEYS, tm), BF16),
                        pltpu.VMEM((d, tm), F32)],
        compiler_params=pltpu.CompilerParams(
            dimension_semantics=("arbitrary", "arbitrary"), vmem_limit_bytes=VMEM_LIMIT),
    )(h2, u_b, vt_b, a, n_sel, bm, r2, x1, gate2, g)


def _block_diag_ones(width, group):
    idx = jnp.arange(width) // group
    return (idx[:, None] == idx[None, :]).astype(BF16)


def kernel(x, c, w_ada, b_ada, norm1_g, w_in, b_f, sg_ln_g, sg_ln_b, sg_w, sg_b, out_g_sg,
           out_g_fox, w_out, norm2_g, w_query, sub_keys, expert_u, expert_v, final_g):
    bsz, seq, d = x.shape
    t = bsz * seq
    depth = w_ada.shape[0]
    x2 = x.reshape(t, d)
    c_uv = 2 * GROUP_W
    c_qkv = c_uv + 3 * GROUP_W
    seg_w = _block_diag_ones(GROUP_W, HEAD_DIM)
    seg_pair = _block_diag_ones(LANE, HEAD_DIM)
    row = lambda a: a.reshape(1, -1)

    for l in range(depth):
        ada = _adaln(c, w_ada[l], b_ada[l])
        shift1, scale1, gate1, shift2, scale2, gate2 = [
            a.reshape(bsz, 1, d) for a in jnp.split(ada, 6, axis=-1)]

        w_in_b = w_in[l].astype(BF16)
        w_ft = jnp.zeros((16, d), BF16).at[:N_HEADS].set(w_in_b[:, c_qkv:].T)
        b_f16 = jnp.zeros((16, 1), F32).at[:N_HEADS, 0].set(b_f[l])
        causal = jnp.tril(jnp.ones((CHUNK, CHUNK), F32))
        w_sg = (sg_w[l] * causal[None]).reshape(N_HEADS * CHUNK, CHUNK).astype(BF16)
        b_sg = jnp.repeat(sg_b[l].T, HEAD_DIM, axis=1)

        ysg, q, k, v, lf = _inproj(
            x2, scale1, shift1, row(norm1_g[l]), w_in_b[:, :c_uv], w_in_b[:, c_uv:c_qkv], w_ft,
            b_f16, row(sg_ln_g[l]), row(sg_ln_b[l]), w_sg, b_sg, row(out_g_sg[l]), seg_w, seq)
        negc = _neg_cumsum(lf, seq)
        yfox = _fox(q, k, v, negc, row(out_g_fox[l]), seg_pair, bsz, seq)

        w_out_b = w_out[l].astype(BF16)
        sk = sub_keys[l].reshape(2 * PEER_HEADS, N_KEYS, -1).astype(BF16)
        x1, h2, st = _outproj(x2, ysg, yfox, gate1, scale2, shift2, row(norm2_g[l]),
                              w_out_b[:GROUP_W], w_out_b[GROUP_W:], w_query[l].astype(BF16),
                              sk, seq)
        a, n_sel, bm, r2 = _peersel(st)
        x2 = _peerffn(h2, expert_u[l].astype(BF16), expert_v[l].T.astype(BF16),
                      a, n_sel, bm, r2, x1, gate2, row(final_g), seq,
                      normalize=(l == depth - 1))
    return x2.reshape(bsz, seq, d)
```

```python
import functools
import math

import jax
import jax.numpy as jnp
import numpy as np
from jax import lax
from jax.experimental import pallas as pl
from jax.experimental.pallas import tpu as pltpu

F32 = jnp.float32
BF16 = jnp.bfloat16

HEAD_DIM = 64
N_HEADS = 8
GROUP_W = N_HEADS * HEAD_DIM
CHUNK = 128
PEER_HEADS = 8
N_KEYS = 128
TOPK = 16
EPS = 1e-6
NEG_BIG = -1e30
RANK_SCALE = 2.0 ** 20
LOG2_E = math.log2(math.e)
VMEM_LIMIT = 48 * 1024 * 1024

TM_PROJ = 512
TM_OUT = 256
TQ = 512
TL_SEL = 256
TM_FFN = 512
E_BLK = 2048
LANE = 128


def _gelu(x):
    return 0.5 * x * (1.0 + lax.erf(x * (1.0 / math.sqrt(2.0))))


def _seg_sum(x, seg):
    hi = x.astype(BF16)
    lo = (x - hi.astype(F32)).astype(BF16)
    return (jnp.dot(hi, seg, preferred_element_type=F32)
            + jnp.dot(lo, seg, preferred_element_type=F32))


def _dot_nt(a, b):
    return lax.dot_general(a, b, (((1,), (1,)), ((), ())), preferred_element_type=F32)


def _adaln_kernel(c_ref, w_ref, b_ref, o_ref):
    c = c_ref[...]
    c_act = c / (1.0 + jnp.exp(-c))
    o_ref[...] = jnp.dot(c_act, w_ref[...], preferred_element_type=F32,
                         precision=lax.Precision.HIGHEST) + b_ref[...]


def _adaln(c, w_ada, b_ada):
    bsz, d = c.shape
    n_out = w_ada.shape[1]
    return pl.pallas_call(
        _adaln_kernel,
        name="adaln",
        grid=(n_out // d,),
        in_specs=[pl.BlockSpec((bsz, d), lambda j: (0, 0)),
                  pl.BlockSpec((d, d), lambda j: (0, j)),
                  pl.BlockSpec((1, d), lambda j: (0, j))],
        out_specs=pl.BlockSpec((bsz, d), lambda j: (0, j)),
        out_shape=jax.ShapeDtypeStruct((bsz, n_out), F32),
        compiler_params=pltpu.CompilerParams(vmem_limit_bytes=VMEM_LIMIT),
    )(c, w_ada, b_ada.reshape(1, n_out))


def _inproj_kernel(x_ref, sc_ref, sh_ref, g1_ref, wuv_ref, wqkv_ref, wft_ref, bf_ref,
                   lng_ref, lnb_ref, wsg_ref, bsg_ref, gsg_ref, seg_ref,
                   ysg_ref, q_ref, k_ref, v_ref, lf_ref):
    x = x_ref[...]
    ms = jnp.mean(x * x, axis=-1, keepdims=True)
    h = x * lax.rsqrt(ms + EPS) * g1_ref[...]
    h = h * (1.0 + sc_ref[0]) + sh_ref[0]
    hb = h.astype(BF16)

    uv = jnp.dot(hb, wuv_ref[...], preferred_element_type=F32)
    zu = _gelu(uv[:, :GROUP_W])
    zv = _gelu(uv[:, GROUP_W:])
    seg = seg_ref[...]
    inv_hd = 1.0 / HEAD_DIM
    mu = _seg_sum(zv, seg) * inv_hd
    dv = zv - mu
    var = _seg_sum(dv * dv, seg) * inv_hd
    vn = (dv * lax.rsqrt(var + EPS) * lng_ref[...] + lnb_ref[...]).astype(BF16)

    lane_head = lax.broadcasted_iota(jnp.int32, (CHUNK, GROUP_W), 1) // HEAD_DIM
    wsg = wsg_ref[...]
    for c in range(x.shape[0] // CHUNK):
        rows = slice(c * CHUNK, (c + 1) * CHUNK)
        r = jnp.dot(wsg, vn[rows], preferred_element_type=F32)
        mixed = bsg_ref[...]
        for hh in range(N_HEADS):
            mixed = mixed + jnp.where(lane_head == hh, r[hh * CHUNK:(hh + 1) * CHUNK], 0.0)
        y = zu[rows] * mixed
        msy = _seg_sum(y * y, seg) * inv_hd
        ysg_ref[rows, :] = (y * lax.rsqrt(msy + EPS) * gsg_ref[...]).astype(BF16)

    qkv = jnp.dot(hb, wqkv_ref[...], preferred_element_type=F32)
    q_ref[...] = (qkv[:, :GROUP_W] * (HEAD_DIM ** -0.5 * LOG2_E)).astype(BF16)
    k_ref[...] = qkv[:, GROUP_W:2 * GROUP_W].astype(BF16)
    v_ref[...] = qkv[:, 2 * GROUP_W:].astype(BF16)

    f = _dot_nt(wft_ref[...], hb) + bf_ref[...]
    lf_ref[...] = jnp.minimum(f, 0.0) - jnp.log1p(jnp.exp(-jnp.abs(f)))


def _inproj(x2, scale1, shift1, g1, w_uv, w_qkv, w_ft, b_f16, ln_g, ln_b, w_sg, b_sg, g_sg,
            seg, seq):
    t, d = x2.shape
    tm = TM_PROJ
    per_b = seq // tm
    full = lambda a: pl.BlockSpec(a.shape, lambda i: (0,) * a.ndim)
    tok = lambda w: pl.BlockSpec((tm, w), lambda i: (i, 0))
    mod = pl.BlockSpec((1, 1, d), lambda i: (i // per_b, 0, 0))
    return pl.pallas_call(
        _inproj_kernel,
        name="inproj",
        grid=(t // tm,),
        in_specs=[tok(d), mod, mod, full(g1), full(w_uv), full(w_qkv), full(w_ft), full(b_f16),
                  full(ln_g), full(ln_b), full(w_sg), full(b_sg), full(g_sg), full(seg)],
        out_specs=[tok(GROUP_W), tok(GROUP_W), tok(GROUP_W), tok(GROUP_W),
                   pl.BlockSpec((16, tm), lambda i: (0, i))],
        out_shape=[jax.ShapeDtypeStruct((t, GROUP_W), BF16)] * 4
                  + [jax.ShapeDtypeStruct((16, t), F32)],
        compiler_params=pltpu.CompilerParams(vmem_limit_bytes=VMEM_LIMIT),
    )(x2, scale1, shift1, g1, w_uv, w_qkv, w_ft, b_f16, ln_g, ln_b, w_sg, b_sg, g_sg, seg)


def _cumsum_kernel(lf_ref, o_ref):
    x = lf_ref[...]
    n = x.shape[1]
    lane = lax.broadcasted_iota(jnp.int32, x.shape, 1)
    k = 1
    while k < n:
        x = x + jnp.where(lane >= k, pltpu.roll(x, k, 1), 0.0)
        k *= 2
    o_ref[...] = x * (-LOG2_E)


def _neg_cumsum(lf, seq):
    r, t = lf.shape
    return pl.pallas_call(
        _cumsum_kernel,
        name="cumsum",
        grid=(t // seq,),
        in_specs=[pl.BlockSpec((r, seq), lambda b: (0, b))],
        out_specs=pl.BlockSpec((r, seq), lambda b: (0, b)),
        out_shape=jax.ShapeDtypeStruct((r, t), F32),
    )(lf)


def _fox_kernel(q_ref, k_ref, v_ref, nc_ref, g_ref, seg_ref, o_ref):
    p = pl.program_id(1)
    qi = pl.program_id(2)
    tq = q_ref.shape[0]
    q = q_ref[...]
    lane = lax.broadcasted_iota(jnp.int32, (1, LANE), 1)
    first = lane < HEAD_DIM
    zero = jnp.zeros_like(q)
    q_heads = (jnp.where(first, q, zero), jnp.where(first, zero, q))
    causal = (lax.broadcasted_iota(jnp.int32, (tq, tq), 1)
              <= lax.broadcasted_iota(jnp.int32, (tq, tq), 0))

    def block(kb, carry, masked):
        start = pl.multiple_of(kb * tq, tq)
        kblk = k_ref[pl.ds(start, tq), :]
        vblk = v_ref[pl.ds(start, tq), :]
        new = []
        for j in range(2):
            m, l, acc = carry[j]
            bias = nc_ref[pl.ds(2 * p + j, 1), pl.ds(start, tq)]
            s = _dot_nt(q_heads[j], kblk) + bias
            if masked:
                s = jnp.where(causal, s, NEG_BIG)
            m_new = jnp.maximum(m, jnp.max(s, axis=1, keepdims=True))
            alpha = jnp.exp2(m - m_new)
            pe = jnp.exp2(s - m_new)
            l = alpha * l + jnp.sum(pe, axis=1, keepdims=True)
            acc = alpha * acc + jnp.dot(pe.astype(BF16), vblk, preferred_element_type=F32)
            new.append((m_new, l, acc))
        return tuple(new)

    init = tuple((jnp.full((tq, 1), NEG_BIG, F32), jnp.zeros((tq, 1), F32),
                  jnp.zeros((tq, LANE), F32)) for _ in range(2))
    carry = lax.fori_loop(0, qi, lambda kb, c: block(kb, c, masked=False), init)
    (_, l0, a0), (_, l1, a1) = block(qi, carry, masked=True)
    o = jnp.where(first, a0 / l0, a1 / l1)
    ms = _seg_sum(o * o, seg_ref[...]) * (1.0 / HEAD_DIM)
    o_ref[...] = (o * lax.rsqrt(ms + EPS) * g_ref[...]).astype(BF16)


def _fox(q, k, v, negc, g_fox, seg, bsz, seq):
    t = q.shape[0]
    nq = seq // TQ
    n_pairs = GROUP_W // LANE
    return pl.pallas_call(
        _fox_kernel,
        name="fox",
        grid=(bsz, n_pairs, nq),
        in_specs=[pl.BlockSpec((TQ, LANE), lambda b, p, i: (b * nq + i, p)),
                  pl.BlockSpec((seq, LANE), lambda b, p, i: (b, p)),
                  pl.BlockSpec((seq, LANE), lambda b, p, i: (b, p)),
                  pl.BlockSpec((negc.shape[0], seq), lambda b, p, i: (0, b)),
                  pl.BlockSpec((1, LANE), lambda b, p, i: (0, p)),
                  pl.BlockSpec((LANE, LANE), lambda b, p, i: (0, 0))],
        out_specs=pl.BlockSpec((TQ, LANE), lambda b, p, i: (b * nq + i, p)),
        out_shape=jax.ShapeDtypeStruct((t, GROUP_W), BF16),
        compiler_params=pltpu.CompilerParams(vmem_limit_bytes=VMEM_LIMIT),
    )(q, k, v, negc, g_fox, seg)


def _outproj_kernel(x_ref, ysg_ref, yfox_ref, gate_ref, sc_ref, sh_ref, g2_ref,
                    wo1_ref, wo2_ref, wq_ref, sk_ref, x1_ref, h2_ref, st_ref):
    mix = (jnp.dot(ysg_ref[...], wo1_ref[...], preferred_element_type=F32)
           + jnp.dot(yfox_ref[...], wo2_ref[...], preferred_element_type=F32))
    x1 = x_ref[...] + gate_ref[0] * mix
    x1_ref[...] = x1
    ms = jnp.mean(x1 * x1, axis=-1, keepdims=True)
    h2 = x1 * lax.rsqrt(ms + EPS) * g2_ref[...]
    h2 = (h2 * (1.0 + sc_ref[0]) + sh_ref[0]).astype(BF16)
    h2_ref[...] = h2
    qp = jnp.dot(h2, wq_ref[...], preferred_element_type=F32).astype(BF16)
    for j in range(2 * PEER_HEADS):
        st_ref[j] = _dot_nt(sk_ref[j], qp[:, j * LANE:(j + 1) * LANE])


def _outproj(x2, ysg, yfox, gate1, scale2, shift2, g2, wo1, wo2, wq, sk, seq):
    t, d = x2.shape
    tm = TM_OUT
    per_b = seq // tm
    full = lambda a: pl.BlockSpec(a.shape, lambda i: (0,) * a.ndim)
    tok = lambda w: pl.BlockSpec((tm, w), lambda i: (i, 0))
    mod = pl.BlockSpec((1, 1, d), lambda i: (i // per_b, 0, 0))
    n_half = sk.shape[0]
    return pl.pallas_call(
        _outproj_kernel,
        name="outproj",
        grid=(t // tm,),
        in_specs=[tok(d), tok(GROUP_W), tok(GROUP_W), mod, mod, mod, full(g2),
                  full(wo1), full(wo2), full(wq), full(sk)],
        out_specs=[tok(d), tok(d), pl.BlockSpec((n_half, N_KEYS, tm), lambda i: (0, 0, i))],
        out_shape=[jax.ShapeDtypeStruct((t, d), F32), jax.ShapeDtypeStruct((t, d), BF16),
                   jax.ShapeDtypeStruct((n_half, N_KEYS, t), F32)],
        compiler_params=pltpu.CompilerParams(vmem_limit_bytes=VMEM_LIMIT),
    )(x2, ysg, yfox, gate1, scale2, shift2, g2, wo1, wo2, wq, sk)


def _ranked_top16(x, pos, tie_safe, want_rank=True):
    n = x.shape[0]
    rank = jnp.full(x.shape, float(TOPK), F32) if want_rank else None
    vals = []
    for r in range(TOPK):
        mx = jnp.max(x, axis=0, keepdims=True)
        hit = x == mx
        if tie_safe:
            first = jnp.min(jnp.where(hit, pos, float(n)), axis=0, keepdims=True)
            hit = pos == first
        vals.append(mx)
        x = jnp.where(hit, -jnp.inf, x)
        if want_rank:
            rank = jnp.where(hit, float(r), rank)
    taken = jnp.sum(jnp.where(x == -jnp.inf, 1.0, 0.0), axis=0, keepdims=True)
    return vals, rank, taken


def _odd_even_merge_sort(n):
    pairs = []
    p = 1
    while p < n:
        k = p
        while k >= 1:
            for j in range(k % p, n - k, 2 * k):
                for i in range(min(k, n - j - k)):
                    if (i + j) // (2 * p) == (i + j + k) // (2 * p):
                        pairs.append((i + j, i + j + k))
            k //= 2
        p *= 2
    return tuple(pairs)


_SUBLANES = 8
_SORT_PAIRS = _odd_even_merge_sort(N_KEYS // _SUBLANES)


def _sorted_top16(x):
    n = x.shape[0] // _SUBLANES
    slabs = [x[_SUBLANES * k:_SUBLANES * (k + 1)] for k in range(n)]
    for i, j in _SORT_PAIRS:
        slabs[i], slabs[j] = jnp.maximum(slabs[i], slabs[j]), jnp.minimum(slabs[i], slabs[j])
    vals = []
    popped = jnp.zeros((1, x.shape[1]), F32)
    for r in range(TOPK):
        mx = jnp.max(slabs[0], axis=0, keepdims=True)
        hit = slabs[0] == mx
        popped = popped + jnp.sum(jnp.where(hit, 1.0, 0.0), axis=0, keepdims=True)
        vals.append(mx)
        last = min(TOPK - r, n) - 1
        for k in range(last):
            slabs[k] = jnp.where(hit, slabs[k + 1], slabs[k])
        slabs[last] = jnp.where(hit, -jnp.inf, slabs[last])
    nxt = jnp.max(slabs[0], axis=0, keepdims=True)
    tied = jnp.where(popped != float(TOPK), 1.0, 0.0) + jnp.where(nxt == vals[-1], 1.0, 0.0)
    for r in range(TOPK - 1):
        tied = tied + jnp.where(vals[r] == vals[r + 1], 1.0, 0.0)
    return vals, tied


_CAND_WIDTH = tuple(TOPK // (r + 1) for r in range(TOPK))
_CAND_ROWS = sum(_CAND_WIDTH)
_CAND_PAD = -(-_CAND_ROWS // 16) * 16


def _peersel_head(h, st_ref, rowsel_ref, a_ref, n_ref, b_ref, r2_ref, tie_safe):
    tl = st_ref.shape[2]
    pos_k = lax.broadcasted_iota(jnp.int32, (N_KEYS, tl), 0).astype(F32)
    pos_c = lax.broadcasted_iota(jnp.int32, (_CAND_PAD, tl), 0).astype(F32)
    pad = jnp.full((_CAND_PAD - _CAND_ROWS, tl), -jnp.inf, F32)
    s1 = st_ref[2 * h]
    s2 = st_ref[2 * h + 1]
    if tie_safe:
        sv1, rank1, _ = _ranked_top16(s1, pos_k, True)
        sv2, rank2, _ = _ranked_top16(s2, pos_k, True)
        tied = jnp.zeros((1, tl), F32)
    else:
        sv1, tied1 = _sorted_top16(s1)
        sv2, tied2 = _sorted_top16(s2)
        tied = tied1 + tied2
        rank1 = None
        rank2 = jnp.full((N_KEYS, tl), float(TOPK), F32)
        for r in range(TOPK):
            rank2 = jnp.where(s2 == sv2[r], float(r), rank2)
    sv2 = jnp.concatenate(sv2, axis=0)
    cand = jnp.concatenate(
        [sv1[r] + sv2[:wdt] for r, wdt in enumerate(_CAND_WIDTH)] + [pad], axis=0)
    _, rank_c, taken_c = _ranked_top16(cand, pos_c, tie_safe)
    tied = tied + jnp.where(taken_c != float(TOPK + _CAND_PAD - _CAND_ROWS), 1.0, 0.0)
    chosen = rank_c < float(TOPK)
    z = jnp.sum(jnp.where(chosen, jnp.exp(cand - (sv1[0] + sv2[0:1])), 0.0),
                axis=0, keepdims=True)
    cnt = jnp.dot(rowsel_ref[...], jnp.where(chosen, 1.0, 0.0).astype(BF16),
                  preferred_element_type=F32)
    n_sel = jnp.zeros((N_KEYS, tl), F32)
    for r in range(TOPK):
        is_r = (rank1 == float(r)) if tie_safe else (s1 == sv1[r])
        n_sel = jnp.where(is_r, cnt[r:r + 1], n_sel)
    a_ref[h] = jnp.exp(s1 - sv1[0]) * (0.5 / z)
    n_ref[h] = n_sel * RANK_SCALE
    b_ref[h] = jnp.exp(s2 - sv2[0:1]).astype(BF16)
    r2_ref[h] = (rank2 * (-RANK_SCALE)).astype(BF16)
    return tied


def _peersel_kernel(st_ref, rowsel_ref, a_ref, n_ref, b_ref, r2_ref):
    for h in range(PEER_HEADS):
        tied = _peersel_head(h, st_ref, rowsel_ref, a_ref, n_ref, b_ref, r2_ref, tie_safe=False)

        @pl.when(jnp.max(tied) > 0.0)
        def _():
            _peersel_head(h, st_ref, rowsel_ref, a_ref, n_ref, b_ref, r2_ref, tie_safe=True)


def _peersel(st):
    n_half, n_keys, t = st.shape
    tl = TL_SEL
    ospec = pl.BlockSpec((PEER_HEADS, n_keys, tl), lambda i: (0, 0, i))
    shape = (PEER_HEADS, n_keys, t)
    row_of = np.full((_CAND_PAD,), -1)
    row_of[:_CAND_ROWS] = np.repeat(np.arange(TOPK), _CAND_WIDTH)
    rowsel = jnp.asarray(row_of[None, :] == np.arange(TOPK)[:, None], BF16)
    return pl.pallas_call(
        _peersel_kernel,
        name="peersel",
        grid=(t // tl,),
        in_specs=[pl.BlockSpec((n_half, n_keys, tl), lambda i: (0, 0, i)),
                  pl.BlockSpec(rowsel.shape, lambda i: (0, 0))],
        out_specs=[ospec] * 4,
        out_shape=[jax.ShapeDtypeStruct(shape, F32), jax.ShapeDtypeStruct(shape, F32),
                   jax.ShapeDtypeStruct(shape, BF16), jax.ShapeDtypeStruct(shape, BF16)],
        compiler_params=pltpu.CompilerParams(vmem_limit_bytes=VMEM_LIMIT),
    )(st, rowsel)


def _peerffn_kernel(h2_ref, u_ref, vt_ref, a_ref, n_ref, b_ref, r2_ref, x1_ref, gate_ref, g_ref,
                    o_ref, wg_ref, bs_ref, rs_ref, acc_ref, *, normalize):
    e = pl.program_id(1)
    tm = h2_ref.shape[0]

    @pl.when(e == 0)
    def _():
        for h in range(PEER_HEADS):
            bs_ref[h] = b_ref[h]
            rs_ref[h] = r2_ref[h]

    pre = _dot_nt(u_ref[...], h2_ref[...])
    zero = jnp.zeros((N_KEYS, LANE), BF16)
    for r in range(u_ref.shape[0] // N_KEYS):
        rows = slice(r * N_KEYS, (r + 1) * N_KEYS)
        for lc in range(tm // LANE):
            lanes = slice(lc * LANE, (lc + 1) * LANE)
            w = None
            for h in range(PEER_HEADS):
                a_row = jnp.broadcast_to(a_ref[h, r:r + 1, lanes], (N_KEYS, LANE)).astype(BF16)
                n_row = jnp.broadcast_to(n_ref[h, r:r + 1, lanes], (N_KEYS, LANE)).astype(BF16)
                gap = n_row + rs_ref[h, :, lanes]
                term = a_row * jnp.minimum(bs_ref[h, :, lanes], jnp.maximum(gap, zero))
                w = term if w is None else w + term
            x = pre[rows, lanes]
            two_gelu = x * (1.0 + lax.erf(x * (1.0 / math.sqrt(2.0))))
            wg_ref[rows, lanes] = w * two_gelu.astype(BF16)
    contrib = jnp.dot(vt_ref[...], wg_ref[...], preferred_element_type=F32)

    @pl.when(e == 0)
    def _():
        acc_ref[...] = contrib

    @pl.when(e != 0)
    def _():
        acc_ref[...] += contrib

    @pl.when(e == pl.num_programs(1) - 1)
    def _():
        x = x1_ref[...] + gate_ref[0] * acc_ref[...].T
        if normalize:
            ms = jnp.mean(x * x, axis=-1, keepdims=True)
            x = x * lax.rsqrt(ms + EPS) * g_ref[...]
        o_ref[...] = x


def _peerffn(h2, u_b, vt_b, a, n_sel, bm, r2, x1, gate2, g, seq, normalize):
    t, d = h2.shape
    n_exp = u_b.shape[0]
    tm = TM_FFN
    per_b = seq // tm
    sel = pl.BlockSpec((PEER_HEADS, N_KEYS, tm), lambda i, e: (0, 0, i))
    sel_row = pl.BlockSpec((PEER_HEADS, E_BLK // N_KEYS, tm), lambda i, e: (0, e, i))
    tok = pl.BlockSpec((tm, d), lambda i, e: (i, 0))
    return pl.pallas_call(
        functools.partial(_peerffn_kernel, normalize=normalize),
        name="peerffn",
        grid=(t // tm, n_exp // E_BLK),
        in_specs=[tok,
                  pl.BlockSpec((E_BLK, d), lambda i, e: (e, 0)),
                  pl.BlockSpec((d, E_BLK), lambda i, e: (0, e)),
                  sel_row, sel_row, sel, sel,
                  tok,
                  pl.BlockSpec((1, 1, d), lambda i, e: (i // per_b, 0, 0)),
                  pl.BlockSpec((1, d), lambda i, e: (0, 0))],
        out_specs=tok,
        out_shape=jax.ShapeDtypeStruct((t, d), F32),
        scratch_shapes=[pltpu.VMEM((E_BLK, tm), BF16),
                        pltpu.VMEM((PEER_HEADS, N_KEYS, tm), BF16),
                        pltpu.VMEM((PEER_HEADS, N_KEYS, tm), BF16),
                        pltpu.VMEM((d, tm), F32)],
        compiler_params=pltpu.CompilerParams(
            dimension_semantics=("arbitrary", "arbitrary"), vmem_limit_bytes=VMEM_LIMIT),
    )(h2, u_b, vt_b, a, n_sel, bm, r2, x1, gate2, g)


def _block_diag_ones(width, group):
    idx = jnp.arange(width) // group
    return (idx[:, None] == idx[None, :]).astype(BF16)


def kernel(x, c, w_ada, b_ada, norm1_g, w_in, b_f, sg_ln_g, sg_ln_b, sg_w, sg_b, out_g_sg,
           out_g_fox, w_out, norm2_g, w_query, sub_keys, expert_u, expert_v, final_g):
    bsz, seq, d = x.shape
    t = bsz * seq
    depth = w_ada.shape[0]
    x2 = x.reshape(t, d)
    c_uv = 2 * GROUP_W
    c_qkv = c_uv + 3 * GROUP_W
    seg_w = _block_diag_ones(GROUP_W, HEAD_DIM)
    seg_pair = _block_diag_ones(LANE, HEAD_DIM)
    row = lambda a: a.reshape(1, -1)

    for l in range(depth):
        ada = _adaln(c, w_ada[l], b_ada[l])
        shift1, scale1, gate1, shift2, scale2, gate2 = [
            a.reshape(bsz, 1, d) for a in jnp.split(ada, 6, axis=-1)]

        w_in_b = w_in[l].astype(BF16)
        w_ft = jnp.zeros((16, d), BF16).at[:N_HEADS].set(w_in_b[:, c_qkv:].T)
        b_f16 = jnp.zeros((16, 1), F32).at[:N_HEADS, 0].set(b_f[l])
        causal = jnp.tril(jnp.ones((CHUNK, CHUNK), F32))
        w_sg = (sg_w[l] * causal[None]).reshape(N_HEADS * CHUNK, CHUNK).astype(BF16)
        b_sg = jnp.repeat(sg_b[l].T, HEAD_DIM, axis=1)

        ysg, q, k, v, lf = _inproj(
            x2, scale1, shift1, row(norm1_g[l]), w_in_b[:, :c_uv], w_in_b[:, c_uv:c_qkv], w_ft,
            b_f16, row(sg_ln_g[l]), row(sg_ln_b[l]), w_sg, b_sg, row(out_g_sg[l]), seg_w, seq)
        negc = _neg_cumsum(lf, seq)
        yfox = _fox(q, k, v, negc, row(out_g_fox[l]), seg_pair, bsz, seq)

        w_out_b = w_out[l].astype(BF16)
        sk = sub_keys[l].reshape(2 * PEER_HEADS, N_KEYS, -1).astype(BF16)
        x1, h2, st = _outproj(x2, ysg, yfox, gate1, scale2, shift2, row(norm2_g[l]),
                              w_out_b[:GROUP_W], w_out_b[GROUP_W:], w_query[l].astype(BF16),
                              sk, seq)
        a, n_sel, bm, r2 = _peersel(st)
        x2 = _peerffn(h2, expert_u[l].astype(BF16), expert_v[l].T.astype(BF16),
                      a, n_sel, bm, r2, x1, gate2, row(final_g), seq,
                      normalize=(l == depth - 1))
    return x2.reshape(bsz, seq, d)
```
